```python
import math
import jax, jax.numpy as jnp
from jax import lax
import numpy as np

D_MODEL = 1024
BATCH = 4
SEQ = 4096
DEPTH = 2
DEC_BATCH = 128
DEC_SEQ = 1
PAST_LEN = 2048
PAGE_SIZE = 128

HEAD_DIM = 64
N_HEADS_A = (D_MODEL // 2) // HEAD_DIM
D_A = N_HEADS_A * HEAD_DIM
D_B = D_MODEL // 4
D_C = D_MODEL // 4
D_MIX = D_A + D_B + D_C
D_IN = 3 * D_A + 2 * D_B + D_C
DILATED = ((128, 1), (512, 4), (2048, 16))
WINDOW_MAX = 2048
BLK = 128
CONV_WIDTH = 31
POOL_WINDOWS = (2, 4, 8, 16)
POOL_GROUP = D_C // len(POOL_WINDOWS)
POOL_PREFIX = max(POOL_WINDOWS) - 1
NUM_BUCKETS = 32
MAX_EXACT = NUM_BUCKETS // 2
REL_MAX_DIST = WINDOW_MAX
D_FF = ((8 * D_MODEL // 3 + 255) // 256) * 256
EPS = 1e-6
NEG_INF = -1e30

kernel_name = 'hybrid_dilated_conv_pool_decoder_step'


def rmsnorm(x, g):
    xf = x.astype(jnp.float32)
    y = xf * lax.rsqrt(jnp.mean(xf * xf, axis=-1, keepdims=True) + EPS)
    return (y * g.astype(jnp.float32)).astype(x.dtype)


def layernorm(x, g, b):
    xf = x.astype(jnp.float32)
    mu = jnp.mean(xf, axis=-1, keepdims=True)
    xc = xf - mu
    y = xc * lax.rsqrt(jnp.mean(xc * xc, axis=-1, keepdims=True) + EPS)
    return (y * g.astype(jnp.float32) + b.astype(jnp.float32)).astype(x.dtype)


def swiglu(x, w_gu, w_down):
    g, u = jnp.split(x @ w_gu, 2, axis=-1)
    return (jax.nn.silu(g) * u) @ w_down


def rel_bucket(dist):
    small = dist < MAX_EXACT
    large = MAX_EXACT + (jnp.log(jnp.maximum(dist, 1).astype(jnp.float32) / MAX_EXACT)
                         / math.log(REL_MAX_DIST / MAX_EXACT) * (NUM_BUCKETS - MAX_EXACT)).astype(jnp.int32)
    return jnp.where(small, dist, jnp.minimum(large, NUM_BUCKETS - 1))


def branch_prompt(q, k, v, rel_bias, window, dil):
    bsz, seq, nh, hd = q.shape
    span = window // dil
    L = seq // dil
    nb = -(-L // BLK)
    Lp = nb * BLK

    def to_res(a):
        return a.reshape(bsz, L, dil, nh, hd).transpose(0, 2, 1, 3, 4).reshape(bsz * dil, L, nh, hd)

    def key_blocks(a):
        a = jnp.pad(to_res(a), ((0, 0), (BLK, Lp - L), (0, 0), (0, 0))).reshape(bsz * dil, nb + 1, BLK, nh, hd)
        return jnp.concatenate([a[:, :-1], a[:, 1:]], axis=2)

    qb = jnp.pad(to_res(q), ((0, 0), (0, Lp - L), (0, 0), (0, 0))).reshape(bsz * dil, nb, BLK, nh, hd)
    kb = key_blocks(k)
    vb = key_blocks(v)
    logits = jnp.einsum('nbqhd,nbkhd->nbhqk', qb, kb, preferred_element_type=jnp.float32)
    qi = jnp.arange(BLK)[:, None]
    kk = jnp.arange(2 * BLK)[None, :]
    dist = BLK + qi - kk
    key_idx = jnp.arange(nb)[:, None] * BLK - BLK + kk
    mask = ((dist >= 0) & (dist <= span))[None] & (key_idx >= 0)[:, None, :]
    bias = rel_bias[rel_bucket(dil * jnp.clip(dist, 0, span))]
    logits = logits + jnp.transpose(bias, (2, 0, 1)).astype(jnp.float32)
    logits = jnp.where(mask[None, :, None], logits, NEG_INF)
    m = jnp.max(logits, axis=-1)
    p = jnp.exp(logits - m[..., None])
    s = jnp.sum(p, axis=-1)
    num = jnp.einsum('nbhqk,nbkhd->nbqhd', p, vb.astype(jnp.float32))

    def from_res(a):
        tail = a.shape[3:]
        a = a.reshape((bsz, dil, Lp) + tail)[:, :, :L]
        return jnp.moveaxis(a, 1, 2).reshape((bsz, seq) + tail)

    return (from_res(jnp.swapaxes(m, 2, 3)), from_res(jnp.swapaxes(s, 2, 3)), from_res(num))


def branch_sample(q, k_buf, v_buf, k_new, v_new, rel_bias, window, dil):
    W = k_buf.shape[1]
    T = q.shape[1]
    span = window // dil
    j = jnp.arange(span + 1)
    idx = W + jnp.arange(T)[:, None] - dil * j[None, :]
    valid = idx >= 0
    from_new = idx >= W

    def gather(buf, new):
        gb = buf[:, jnp.clip(idx, 0, W - 1)]
        gn = new[:, jnp.clip(idx - W, 0, T - 1)]
        return jnp.where(from_new[None, :, :, None, None], gn, gb)

    kg = gather(k_buf, k_new)
    vg = gather(v_buf, v_new)
    logits = jnp.einsum('bthd,btjhd->bthj', q, kg, preferred_element_type=jnp.float32)
    bias = rel_bias[rel_bucket(dil * j)]
    logits = logits + bias.T.astype(jnp.float32)
    logits = jnp.where(valid[None, :, None, :], logits, NEG_INF)
    m = jnp.max(logits, axis=-1)
    p = jnp.exp(logits - m[..., None])
    s = jnp.sum(p, axis=-1)
    num = jnp.einsum('bthj,btjhd->bthd', p, vg.astype(jnp.float32))
    return m, s, num


def merge_branches(results):
    big = jnp.max(jnp.stack([r[0] for r in results]), axis=0)
    num = 0.0
    den = 0.0
    for m, s, n in results:
        w = jnp.exp(m - big)
        num = num + w[..., None] * n
        den = den + w * s
    return num / den[..., None]


def conv_module(glu_ext, w_dw, b_dw, ln_g, ln_b, w_pw):
    y = lax.conv_general_dilated(glu_ext, w_dw[:, None, :], window_strides=(1,), padding='VALID',
                                 dimension_numbers=('NWC', 'WIO', 'NWC'), feature_group_count=D_B) + b_dw
    y = layernorm(y, ln_g, ln_b)
    return jax.nn.silu(y) @ w_pw


def pool_mixer(c_ext, pos0, w_pool, scale):
    bsz, n_ext, _ = c_ext.shape
    P = POOL_PREFIX
    T = n_ext - P
    cf = c_ext.astype(jnp.float32)
    cs = jnp.concatenate([jnp.zeros((bsz, 1, D_C), jnp.float32), jnp.cumsum(cf, axis=1)], axis=1)
    end = cs[:, P + 1:]
    pos = pos0 + jnp.arange(T)
    outs = []
    for g, w in enumerate(POOL_WINDOWS):
        sl = slice(g * POOL_GROUP, (g + 1) * POOL_GROUP)
        start = cs[:, P + 1 - w:P + 1 - w + T, sl]
        cnt = jnp.minimum(w, pos + 1).astype(jnp.float32)[None, :, None]
        outs.append((end[..., sl] - start) / cnt - cf[:, P:, sl])
    d = jnp.stack(outs, axis=2)
    y = jnp.einsum('btgc,gce->btge', d, w_pool.astype(jnp.float32)).reshape(bsz, T, D_C)
    return (y * scale.astype(jnp.float32)).astype(c_ext.dtype)


def layer_pre(h, g_f1, w_f1_gu, w_f1_down, g_mix, w_in, g_q, g_k):
    h = h + 0.5 * swiglu(rmsnorm(h, g_f1), w_f1_gu, w_f1_down)
    u = rmsnorm(h, g_mix) @ w_in
    bsz, T, _ = u.shape
    q, k, v, b_val, b_gate, c = jnp.split(
        u, [D_A, 2 * D_A, 3 * D_A, 3 * D_A + D_B, 3 * D_A + 2 * D_B], axis=-1)
    q = rmsnorm(q.reshape(bsz, T, N_HEADS_A, HEAD_DIM), g_q) * (HEAD_DIM ** -0.5)
    k = rmsnorm(k.reshape(bsz, T, N_HEADS_A, HEAD_DIM), g_k)
    v = v.reshape(bsz, T, N_HEADS_A, HEAD_DIM)
    glu = b_val * jax.nn.sigmoid(b_gate)
    return h, q, k, v, glu, c


def layer_post(h, attn, conv_out, pool_out, w_out, g_f2, w_f2_gu, w_f2_down):
    bsz, T, _ = h.shape
    mix = jnp.concatenate([attn.reshape(bsz, T, D_A).astype(h.dtype), conv_out, pool_out], axis=-1) @ w_out
    h = h + mix
    return h + 0.5 * swiglu(rmsnorm(h, g_f2), w_f2_gu, w_f2_down)


def setup_inputs(seed: int = 0) -> dict:
    key = jax.random.key(seed)
    ks = jax.random.split(key, 32)
    f32 = jnp.float32

    def nrm(k, shape, scale):
        return jax.random.normal(k, shape, f32) * scale

    w_buf = min(WINDOW_MAX, PAST_LEN)
    return {
        'x_prompt': nrm(ks[0], (BATCH, SEQ, D_MODEL), 1.0),
        'x_sample': nrm(ks[1], (DEC_BATCH, DEC_SEQ, D_MODEL), 1.0),
        'cache_attn_k': nrm(ks[2], (DEPTH, DEC_BATCH, w_buf, N_HEADS_A, HEAD_DIM), 1.0),
        'cache_attn_v': nrm(ks[3], (DEPTH, DEC_BATCH, w_buf, N_HEADS_A, HEAD_DIM), 1.0),
        'cache_conv': nrm(ks[4], (DEPTH, DEC_BATCH, CONV_WIDTH - 1, D_B), 0.5),
        'cache_pool': nrm(ks[5], (DEPTH, DEC_BATCH, POOL_PREFIX, D_C), 1.0),
        'rel_bias': nrm(ks[6], (NUM_BUCKETS, N_HEADS_A), 0.5),
        'g_ffn1': 1.0 + nrm(ks[7], (DEPTH, D_MODEL), 0.05),
        'w_ffn1_gu': nrm(ks[8], (DEPTH, D_MODEL, 2 * D_FF), D_MODEL ** -0.5),
        'w_ffn1_down': nrm(ks[9], (DEPTH, D_FF, D_MODEL), D_FF ** -0.5),
        'g_mix': 1.0 + nrm(ks[10], (DEPTH, D_MODEL), 0.05),
        'w_in': nrm(ks[11], (DEPTH, D_MODEL, D_IN), D_MODEL ** -0.5),
        'g_q': 1.0 + nrm(ks[12], (DEPTH, HEAD_DIM), 0.05),
        'g_k': 1.0 + nrm(ks[13], (DEPTH, HEAD_DIM), 0.05),
        'conv_w': nrm(ks[14], (DEPTH, CONV_WIDTH, D_B), CONV_WIDTH ** -0.5),
        'conv_b': nrm(ks[15], (DEPTH, D_B), 0.02),
        'conv_ln_g': 1.0 + nrm(ks[16], (DEPTH, D_B), 0.05),
        'conv_ln_b': nrm(ks[17], (DEPTH, D_B), 0.02),
        'conv_pw': nrm(ks[18], (DEPTH, D_B, D_B), D_B ** -0.5),
        'pool_w': nrm(ks[19], (DEPTH, len(POOL_WINDOWS), POOL_GROUP, POOL_GROUP), POOL_GROUP ** -0.5),
        'pool_scale': 1.0 + nrm(ks[20], (DEPTH, D_C), 0.1),
        'w_out': nrm(ks[21], (DEPTH, D_MIX, D_MODEL), D_MIX ** -0.5),
        'g_ffn2': 1.0 + nrm(ks[22], (DEPTH, D_MODEL), 0.05),
        'w_ffn2_gu': nrm(ks[23], (DEPTH, D_MODEL, 2 * D_FF), D_MODEL ** -0.5),
        'w_ffn2_down': nrm(ks[24], (DEPTH, D_FF, D_MODEL), D_FF ** -0.5),
    }


def reference(x_prompt, x_sample, cache_attn_k, cache_attn_v, cache_conv, cache_pool, rel_bias,
              g_ffn1, w_ffn1_gu, w_ffn1_down, g_mix, w_in, g_q, g_k, conv_w, conv_b, conv_ln_g,
              conv_ln_b, conv_pw, pool_w, pool_scale, w_out, g_ffn2, w_ffn2_gu, w_ffn2_down):
    hp = x_prompt
    hs = x_sample
    pk, pv, pc, pp = [], [], [], []
    sk, sv, sc, sp = [], [], [], []
    for l in range(DEPTH):
        pre = (g_ffn1[l], w_ffn1_gu[l], w_ffn1_down[l], g_mix[l], w_in[l], g_q[l], g_k[l])
        conv_p = (conv_w[l], conv_b[l], conv_ln_g[l], conv_ln_b[l], conv_pw[l])
        post = (w_out[l], g_ffn2[l], w_ffn2_gu[l], w_ffn2_down[l])

        hp, q, k, v, glu, c = layer_pre(hp, *pre)
        attn = merge_branches([branch_prompt(q, k, v, rel_bias, w, d) for (w, d) in DILATED])
        bsz = hp.shape[0]
        glu_ext = jnp.concatenate([jnp.zeros((bsz, CONV_WIDTH - 1, D_B), glu.dtype), glu], axis=1)
        c_ext = jnp.concatenate([jnp.zeros((bsz, POOL_PREFIX, D_C), c.dtype), c], axis=1)
        conv_out = conv_module(glu_ext, *conv_p)
        pool_out = pool_mixer(c_ext, 0, pool_w[l], pool_scale[l])
        hp = layer_post(hp, attn, conv_out, pool_out, *post)
        keep = min(WINDOW_MAX, k.shape[1])
        pk.append(k[:, -keep:])
        pv.append(v[:, -keep:])
        pc.append(glu_ext[:, -(CONV_WIDTH - 1):])
        pp.append(c_ext[:, -POOL_PREFIX:])

        hs, q, k, v, glu, c = layer_pre(hs, *pre)
        k_buf = cache_attn_k[l]
        v_buf = cache_attn_v[l]
        attn = merge_branches([branch_sample(q, k_buf, v_buf, k, v, rel_bias, w, d) for (w, d) in DILATED])
        glu_ext = jnp.concatenate([cache_conv[l], glu], axis=1)
        c_ext = jnp.concatenate([cache_pool[l], c], axis=1)
        conv_out = conv_module(glu_ext, *conv_p)
        pool_out = pool_mixer(c_ext, PAST_LEN, pool_w[l], pool_scale[l])
        hs = layer_post(hs, attn, conv_out, pool_out, *post)
        sk.append(k)
        sv.append(v)
        sc.append(glu_ext[:, -(CONV_WIDTH - 1):])
        sp.append(c_ext[:, -POOL_PREFIX:])

    return (hp, hs, jnp.stack(pk), jnp.stack(pv), jnp.stack(pc), jnp.stack(pp),
            jnp.stack(sk), jnp.stack(sv), jnp.stack(sc), jnp.stack(sp))
```

```python
import functools
import math

import numpy as np
import jax
import jax.numpy as jnp
from jax import lax
from jax.experimental import pallas as pl
from jax.experimental.pallas import tpu as pltpu

F32 = jnp.float32
BF16 = jnp.bfloat16

D_MODEL = 1024
HEAD_DIM = 64
N_HEADS = 8
D_A = N_HEADS * HEAD_DIM
D_B = 256
D_C = 256
D_IN = 3 * D_A + 2 * D_B + D_C
D_FF = 2816
DILATIONS = (1, 4, 16)
SPAN = 128
BLK = 128
CONV_WIDTH = 31
POOL_WINDOWS = (2, 4, 8, 16)
POOL_GROUP = 64
POOL_PREFIX = 15
NUM_BUCKETS = 32
MAX_EXACT = 16
REL_MAX_DIST = 2048
EPS = 1e-6
NEG_INF = -1e30

HEADS_PER_VREG = 128 // HEAD_DIM
N_HEAD_PAIRS = N_HEADS // HEADS_PER_VREG
HALO = 32
FF_CHUNK = 256
VMEM_LIMIT = 56 * 1024 * 1024


def _bucket_np(dist):
    dist = np.asarray(dist, np.int64)
    ratio = np.log(np.maximum(dist, 1).astype(np.float32) / np.float32(MAX_EXACT))
    large = MAX_EXACT + (ratio / np.float32(math.log(REL_MAX_DIST / MAX_EXACT))
                         * np.float32(NUM_BUCKETS - MAX_EXACT)).astype(np.int32)
    return np.where(dist < MAX_EXACT, dist, np.minimum(large, NUM_BUCKETS - 1)).astype(np.int32)


def _prompt_bucket_table():
    qi = np.arange(BLK)[:, None]
    kk = np.arange(2 * BLK)[None, :]
    dist = BLK + qi - kk
    ok = (dist >= 0) & (dist <= SPAN)
    tabs = []
    for dil in DILATIONS:
        b = _bucket_np(dil * np.clip(dist, 0, SPAN))
        tabs.append(np.where(ok, b, -1))
    return np.stack(tabs).astype(np.int32)


def _sample_tables(wbuf):
    dist = wbuf - np.arange(wbuf)
    count = np.zeros(wbuf, np.int32)
    for dil in DILATIONS:
        count += ((dist % dil == 0) & (dist // dil <= SPAN)).astype(np.int32)
    bucket = np.where(count > 0, _bucket_np(dist), -1).astype(np.int32)
    return bucket[None, :], count.astype(np.float32)[None, :]


def _dot(a, b):
    return jnp.dot(a, b, preferred_element_type=F32)


def _rms(x, g):
    ms = jnp.mean(x * x, axis=-1, keepdims=True)
    return x * lax.rsqrt(ms + EPS) * g


def _silu(x):
    return x * jax.nn.sigmoid(x)


def _swiglu_residual(h, g_ref, wgu_ref, wdn_ref):
    xn = _rms(h, g_ref[...]).astype(BF16)
    acc = None
    for c in range(D_FF // FF_CHUNK):
        lo = c * FF_CHUNK
        g = _dot(xn, wgu_ref[:, lo:lo + FF_CHUNK])
        u = _dot(xn, wgu_ref[:, D_FF + lo:D_FF + lo + FF_CHUNK])
        a = (_silu(g) * u).astype(BF16)
        d = _dot(a, wdn_ref[lo:lo + FF_CHUNK, :])
        acc = d if acc is None else acc + d
    return h + 0.5 * acc


def _head_rms(x, bd_ref, g):
    ms = _dot((x * x).astype(BF16), bd_ref[...])
    return x * lax.rsqrt(ms + EPS) * g


def _layernorm(x, g, b):
    mu = jnp.mean(x, axis=-1, keepdims=True)
    xc = x - mu
    var = jnp.mean(xc * xc, axis=-1, keepdims=True)
    return xc * lax.rsqrt(var + EPS) * g + b


def _pre_kernel(h_ref, gf_ref, wgu_ref, wdn_ref, gm_ref, win_ref, gq_ref, gk_ref, bd_ref,
                h1_ref, q_ref, k_ref, v_ref, glu_ref, c_ref):
    h1 = _swiglu_residual(h_ref[...], gf_ref, wgu_ref, wdn_ref)
    h1_ref[...] = h1
    xn = _rms(h1, gm_ref[...]).astype(BF16)
    q = _dot(xn, win_ref[:, 0:D_A])
    q_ref[...] = _head_rms(q, bd_ref, gq_ref[...]) * (HEAD_DIM ** -0.5)
    k = _dot(xn, win_ref[:, D_A:2 * D_A])
    k_ref[...] = _head_rms(k, bd_ref, gk_ref[...])
    v_ref[...] = _dot(xn, win_ref[:, 2 * D_A:3 * D_A])
    o = 3 * D_A
    b_val = _dot(xn, win_ref[:, o:o + D_B])
    b_gate = _dot(xn, win_ref[:, o + D_B:o + 2 * D_B])
    glu_ref[...] = b_val * jax.nn.sigmoid(b_gate)
    c_ref[...] = _dot(xn, win_ref[:, o + 2 * D_B:o + 2 * D_B + D_C])


def _const_spec(shape):
    nd = len(shape)
    return pl.BlockSpec(shape, lambda i, _nd=nd: (0,) * _nd, pipeline_mode=pl.Buffered(1))


def _pre_call(h, gf, wgu, wdn, gm, win, gq, gk, bd, tm):
    t = h.shape[0]
    row = lambda w: pl.BlockSpec((tm, w), lambda i: (i, 0))
    return pl.pallas_call(
        _pre_kernel,
        grid=(t // tm,),
        in_specs=[row(D_MODEL), _const_spec((1, D_MODEL)), _const_spec((D_MODEL, 2 * D_FF)),
                  _const_spec((D_FF, D_MODEL)), _const_spec((1, D_MODEL)),
                  _const_spec((D_MODEL, D_IN)), _const_spec((1, D_A)), _const_spec((1, D_A)),
                  _const_spec((D_A, D_A))],
        out_specs=[row(D_MODEL), row(D_A), row(D_A), row(D_A), row(D_B), row(D_C)],
        out_shape=[jax.ShapeDtypeStruct((t, w), F32) for w in (D_MODEL, D_A, D_A, D_A, D_B, D_C)],
        compiler_params=pltpu.CompilerParams(dimension_semantics=("arbitrary",),
                                             vmem_limit_bytes=VMEM_LIMIT),
        name="pre_ffn_inproj",
    )(h, gf, wgu, wdn, gm, win, gq, gk, bd)


def _post_kernel(h_ref, attn_ref, conv_ref, pool_ref, wout_ref, gf_ref, wgu_ref, wdn_ref, out_ref):
    mix = _dot(attn_ref[...].astype(BF16), wout_ref[0:D_A, :])
    mix = mix + _dot(conv_ref[...].astype(BF16), wout_ref[D_A:D_A + D_B, :])
    mix = mix + _dot(pool_ref[...].astype(BF16), wout_ref[D_A + D_B:D_A + D_B + D_C, :])
    h2 = h_ref[...] + mix
    out_ref[...] = _swiglu_residual(h2, gf_ref, wgu_ref, wdn_ref)


def _post_call(h, attn, conv, pool, wout, gf, wgu, wdn, tm):
    t = h.shape[0]
    row = lambda w: pl.BlockSpec((tm, w), lambda i: (i, 0))
    return pl.pallas_call(
        _post_kernel,
        grid=(t // tm,),
        in_specs=[row(D_MODEL), row(D_A), row(D_B), row(D_C),
                  _const_spec((D_MODEL, D_MODEL)), _const_spec((1, D_MODEL)),
                  _const_spec((D_MODEL, 2 * D_FF)), _const_spec((D_FF, D_MODEL))],
        out_specs=row(D_MODEL),
        out_shape=jax.ShapeDtypeStruct((t, D_MODEL), F32),
        compiler_params=pltpu.CompilerParams(dimension_semantics=("arbitrary",),
                                             vmem_limit_bytes=VMEM_LIMIT),
        name="post_mix_ffn",
    )(h, attn, conv, pool, wout, gf, wgu, wdn)


def _bias_prompt_kernel(rb_ref, tbl_ref, out_ref):
    h = pl.program_id(1)
    tbl = tbl_ref[0]

    def body(b, acc):
        return jnp.where(tbl == b, rb_ref[b, h], acc)

    acc = lax.fori_loop(0, NUM_BUCKETS, body, jnp.zeros(tbl.shape, F32))
    out_ref[0, 0] = jnp.where(tbl < 0, NEG_INF, acc)


def _bias_prompt_call(rel_bias):
    tbl = jnp.asarray(_prompt_bucket_table())
    nbr = len(DILATIONS)
    return pl.pallas_call(
        _bias_prompt_kernel,
        grid=(nbr, N_HEADS),
        in_specs=[pl.BlockSpec(memory_space=pltpu.SMEM),
                  pl.BlockSpec((1, BLK, 2 * BLK), lambda br, h: (br, 0, 0))],
        out_specs=pl.BlockSpec((1, 1, BLK, 2 * BLK), lambda br, h: (br, h, 0, 0)),
        out_shape=jax.ShapeDtypeStruct((nbr, N_HEADS, BLK, 2 * BLK), F32),
        name="bias_prompt",
    )(rel_bias, tbl)


def _bias_sample_kernel(rb_ref, tbl_ref, out_ref):
    tbl = tbl_ref[...]
    for h in range(N_HEADS):
        def body(b, acc, h=h):
            return jnp.where(tbl == b, rb_ref[b, h], acc)

        acc = lax.fori_loop(0, NUM_BUCKETS, body, jnp.zeros(tbl.shape, F32))
        out_ref[h:h + 1, :] = jnp.where(tbl < 0, NEG_INF, acc)


def _bias_sample_call(rel_bias, bucket):
    return pl.pallas_call(
        _bias_sample_kernel,
        in_specs=[pl.BlockSpec(memory_space=pltpu.SMEM),
                  pl.BlockSpec(bucket.shape, lambda: (0, 0))],
        out_shape=jax.ShapeDtypeStruct((N_HEADS, bucket.shape[1]), F32),
        name="bias_sample",
    )(rel_bias, bucket)


def _rows(start, size, dil):
    return pl.ds(start, size) if dil == 1 else pl.ds(start, size, stride=dil)


def _attn_prompt_kernel(q_ref, k_ref, v_ref, bias_ref, o_ref, m_acc, s_acc, n_acc, *, seq):
    lane_lo = lax.broadcasted_iota(jnp.int32, (BLK, 128), 1) < HEAD_DIM

    def unit(br, dil, q_start, k_start, n_keys, last):
        q = q_ref[0, _rows(q_start, BLK, dil), :]
        qs = jnp.concatenate([jnp.where(lane_lo, q, 0.0), jnp.where(lane_lo, 0.0, q)], axis=0)
        kb = k_ref[0, _rows(k_start, n_keys, dil), :].astype(BF16)
        vb = v_ref[0, _rows(k_start, n_keys, dil), :].astype(BF16)
        lg = lax.dot_general(qs.astype(BF16), kb, (((1,), (1,)), ((), ())),
                             preferred_element_type=F32)
        lg = lg + bias_ref[br, 0, :, 2 * BLK - n_keys:]
        m = jnp.max(lg, axis=-1, keepdims=True)
        p = jnp.exp(lg - m)
        s = jnp.sum(p, axis=-1, keepdims=True)
        pv = _dot(p.astype(BF16), vb)
        num = jnp.where(lane_lo, pv[:BLK], pv[BLK:])
        mm = jnp.where(lane_lo, m[:BLK], m[BLK:])
        ss = jnp.where(lane_lo, s[:BLK], s[BLK:])
        rows = _rows(q_start, BLK, dil)
        if br == 0:
            m_acc[rows, :] = mm
            s_acc[rows, :] = ss
            n_acc[rows, :] = num
            return
        m_old = m_acc[rows, :]
        m_new = jnp.maximum(m_old, mm)
        a = jnp.exp(m_old - m_new)
        b = jnp.exp(mm - m_new)
        s_new = a * s_acc[rows, :] + b * ss
        n_new = a * n_acc[rows, :] + b * num
        if last:
            o_ref[0, rows, :] = n_new / s_new
        else:
            m_acc[rows, :] = m_new
            s_acc[rows, :] = s_new
            n_acc[rows, :] = n_new

    for br, dil in enumerate(DILATIONS):
        last = br == len(DILATIONS) - 1
        nb = seq // dil // BLK

        def first_block(r, carry, br=br, dil=dil, last=last):
            unit(br, dil, r, r, BLK, last)
            return carry

        def later_block(u, carry, br=br, dil=dil, last=last):
            lb = 1 + u // dil
            r = u % dil
            unit(br, dil, r + dil * BLK * lb, r + dil * BLK * (lb - 1), 2 * BLK, last)
            return carry

        lax.fori_loop(0, dil, first_block, 0)
        lax.fori_loop(0, dil * (nb - 1), later_block, 0)


def _attn_prompt_call(q, k, v, bias2):
    bsz, seq, _ = q.shape
    blk = pl.BlockSpec((1, seq, 128), lambda b, hp: (b, 0, hp))
    return pl.pallas_call(
        functools.partial(_attn_prompt_kernel, seq=seq),
        grid=(bsz, N_HEAD_PAIRS),
        in_specs=[blk, blk, blk,
                  pl.BlockSpec((len(DILATIONS), 1, HEADS_PER_VREG * BLK, 2 * BLK),
                               lambda b, hp: (0, hp, 0, 0))],
        out_specs=blk,
        out_shape=jax.ShapeDtypeStruct((bsz, seq, D_A), F32),
        scratch_shapes=[pltpu.VMEM((seq, 128), F32)] * 3,
        compiler_params=pltpu.CompilerParams(dimension_semantics=("arbitrary", "arbitrary"),
                                             vmem_limit_bytes=VMEM_LIMIT),
        name="attn_prompt",
    )(q, k, v, bias2)


def _pool_lane_consts(shape):
    grp = lax.broadcasted_iota(jnp.int32, shape, len(shape) - 1) // POOL_GROUP
    win = jnp.full(shape, POOL_WINDOWS[0], jnp.int32)
    for g in range(1, len(POOL_WINDOWS)):
        win = jnp.where(grp == g, POOL_WINDOWS[g], win)
    return win


def _conv_tail(y, cb_ref, lng_ref, lnb_ref, pw_ref):
    y = _layernorm(y + cb_ref[...], lng_ref[...], lnb_ref[...])
    return _dot(_silu(y).astype(BF16), pw_ref[...])


def _mix_prompt_kernel(g_ref, gh_ref, c_ref, ch_ref, cw_ref, cb_ref, lng_ref, lnb_ref, pw_ref,
                       plw_ref, psc_ref, conv_ref, pool_ref, gx, cx, *, tc):
    i = pl.program_id(1)
    keep = (i > 0).astype(F32)
    gx[0:HALO, :] = gh_ref[0] * keep
    gx[HALO:HALO + tc, :] = g_ref[0]
    cx[0:HALO, :] = ch_ref[0] * keep
    cx[HALO:HALO + tc, :] = c_ref[0]

    base = HALO - (CONV_WIDTH - 1)
    y = None
    for j in range(CONV_WIDTH):
        term = cw_ref[j:j + 1, :] * gx[base + j:base + j + tc, :]
        y = term if y is None else y + term
    conv_ref[0] = _conv_tail(y, cb_ref, lng_ref, lnb_ref, pw_ref)

    x = c_ref[0]
    win = _pool_lane_consts((tc, D_C))
    run = x
    tot = jnp.zeros_like(x)
    prev_w = 1
    for w in POOL_WINDOWS:
        for sft in range(prev_w, w):
            run = run + cx[HALO - sft:HALO - sft + tc, :]
        tot = jnp.where(win == w, run, tot)
        prev_w = w
    pos = i * tc + lax.broadcasted_iota(jnp.int32, (tc, D_C), 0)
    cnt = jnp.minimum(win, pos + 1).astype(F32)
    d = tot / cnt - x
    pool_ref[0] = _dot(d.astype(BF16), plw_ref[...]) * psc_ref[...]


def _mix_prompt_call(glu, c, cw, cb, lng, lnb, pw, plw, psc, tc):
    bsz, seq, _ = glu.shape
    per = tc // HALO
    cur = pl.BlockSpec((1, tc, D_B), lambda b, i: (b, i, 0))
    halo = pl.BlockSpec((1, HALO, D_B), lambda b, i: (b, jnp.maximum(i * per - 1, 0), 0))
    cst = lambda shape: pl.BlockSpec(shape, lambda b, i: (0,) * len(shape))
    return pl.pallas_call(
        functools.partial(_mix_prompt_kernel, tc=tc),
        grid=(bsz, seq // tc),
        in_specs=[cur, halo, cur, halo, cst((CONV_WIDTH, D_B)), cst((1, D_B)), cst((1, D_B)),
                  cst((1, D_B)), cst((D_B, D_B)), cst((D_C, D_C)), cst((1, D_C))],
        out_specs=[cur, cur],
        out_shape=[jax.ShapeDtypeStruct((bsz, seq, D_B), F32),
                   jax.ShapeDtypeStruct((bsz, seq, D_C), F32)],
        scratch_shapes=[pltpu.VMEM((HALO + tc, D_B), F32), pltpu.VMEM((HALO + tc, D_C), F32)],
        compiler_params=pltpu.CompilerParams(dimension_semantics=("arbitrary", "arbitrary")),
        name="mix_prompt",
    )(glu, glu, c, c, cw, cb, lng, lnb, pw, plw, psc)


def _attn_sample_kernel(qt_ref, kt_ref, vt_ref, ck_ref, cv_ref, sbias_ref, cnt_ref, rb0_ref, o_ref):
    i = pl.program_id(0)
    sel = lax.broadcasted_iota(jnp.int32, (D_A, 128), 1) == i

    def column(ref):
        return jnp.sum(jnp.where(sel, ref[...], 0.0), axis=1, keepdims=True)

    qc, kc, vc = column(qt_ref), column(kt_ref), column(vt_ref)
    head = lambda a, h: a[h * HEAD_DIM:(h + 1) * HEAD_DIM]
    rows, new = [], []
    for h in range(N_HEADS):
        rows.append(jnp.sum(ck_ref[0, 0, h] * head(qc, h), axis=0, keepdims=True))
        new.append(jnp.sum(head(qc, h) * head(kc, h), axis=0, keepdims=True))
    lg = jnp.concatenate(rows, axis=0) + sbias_ref[...]
    lg0 = jnp.concatenate(new, axis=0) + rb0_ref[...]
    m = jnp.maximum(jnp.max(lg, axis=1, keepdims=True), lg0)
    p = jnp.exp(lg - m) * cnt_ref[...]
    p0 = len(DILATIONS) * jnp.exp(lg0 - m)
    den = jnp.sum(p, axis=1, keepdims=True) + p0
    cols = []
    for h in range(N_HEADS):
        num = jnp.sum(cv_ref[0, 0, h] * p[h:h + 1, :], axis=1, keepdims=True)
        num = num + p0[h:h + 1] * head(vc, h)
        cols.append(num / den[h:h + 1])
    out_col = jnp.concatenate(cols, axis=0)

    @pl.when(i == 0)
    def _():
        o_ref[...] = jnp.zeros(o_ref.shape, F32)

    o_ref[...] = jnp.where(sel, out_col, o_ref[...])


def _attn_sample_call(layer, qt, kt, vt, ck, cv, sbias, cnt, rb0):
    nseq = qt.shape[1]
    wbuf = ck.shape[-1]
    cst = lambda shape: pl.BlockSpec(shape, lambda i: (0,) * len(shape))
    cache = pl.BlockSpec((1, 1, N_HEADS, HEAD_DIM, wbuf), lambda i: (layer, i, 0, 0, 0))
    return pl.pallas_call(
        _attn_sample_kernel,
        grid=(nseq,),
        in_specs=[cst((D_A, nseq))] * 3 + [cache, cache, cst((N_HEADS, wbuf)), cst((1, wbuf)),
                                           cst((N_HEADS, 1))],
        out_specs=cst((D_A, nseq)),
        out_shape=jax.ShapeDtypeStruct((D_A, nseq), F32),
        compiler_params=pltpu.CompilerParams(dimension_semantics=("arbitrary",),
                                             vmem_limit_bytes=VMEM_LIMIT),
        name="attn_sample",
    )(qt, kt, vt, ck, cv, sbias, cnt, rb0)


def _mix_sample_kernel(glu_ref, c_ref, cc_ref, cp_ref, cw_ref, cb_ref, lng_ref, lnb_ref, pw_ref,
                       plw_ref, psc_ref, conv_ref, pool_ref):
    n_hist = CONV_WIDTH - 1
    y = glu_ref[...] * cw_ref[n_hist:n_hist + 1, :]
    for j in range(n_hist):
        y = y + cc_ref[0, j] * cw_ref[j:j + 1, :]
    conv_ref[...] = _conv_tail(y, cb_ref, lng_ref, lnb_ref, pw_ref)

    x = c_ref[...]
    win = _pool_lane_consts(x.shape)
    run = x
    tot = jnp.zeros_like(x)
    prev_w = 1
    for w in POOL_WINDOWS:
        for sft in range(prev_w, w):
            run = run + cp_ref[0, POOL_PREFIX - sft]
        tot = jnp.where(win == w, run, tot)
        prev_w = w
    d = tot / win.astype(F32) - x
    pool_ref[...] = _dot(d.astype(BF16), plw_ref[...]) * psc_ref[...]


def _mix_sample_call(layer, glu, c, cct, cpt, cw, cb, lng, lnb, pw, plw, psc):
    nseq = glu.shape[0]
    cst = lambda shape: pl.BlockSpec(shape, lambda i: (0,) * len(shape))
    return pl.pallas_call(
        _mix_sample_kernel,
        grid=(1,),
        in_specs=[cst((nseq, D_B)), cst((nseq, D_C)),
                  pl.BlockSpec((1, CONV_WIDTH - 1, nseq, D_B), lambda i: (layer, 0, 0, 0)),
                  pl.BlockSpec((1, POOL_PREFIX, nseq, D_C), lambda i: (layer, 0, 0, 0)),
                  cst((CONV_WIDTH, D_B)), cst((1, D_B)), cst((1, D_B)), cst((1, D_B)),
                  cst((D_B, D_B)), cst((D_C, D_C)), cst((1, D_C))],
        out_specs=[cst((nseq, D_B)), cst((nseq, D_C))],
        out_shape=[jax.ShapeDtypeStruct((nseq, D_B), F32), jax.ShapeDtypeStruct((nseq, D_C), F32)],
        compiler_params=pltpu.CompilerParams(dimension_semantics=("arbitrary",)),
        name="mix_sample",
    )(glu, c, cct, cpt, cw, cb, lng, lnb, pw, plw, psc)


def _block_diag(blocks):
    n, r, c = blocks.shape
    eye = jnp.eye(n, dtype=blocks.dtype)
    return (eye[:, None, :, None] * blocks[:, :, None, :]).reshape(n * r, n * c)


def kernel(x_prompt, x_sample, cache_attn_k, cache_attn_v, cache_conv, cache_pool, rel_bias, g_ffn1, w_ffn1_gu, w_ffn1_down, g_mix, w_in, g_q, g_k, conv_w, conv_b, conv_ln_g, conv_ln_b, conv_pw, pool_w, pool_scale, w_out, g_ffn2, w_ffn2_gu, w_ffn2_down):
    bsz, seq, _ = x_prompt.shape
    nseq = x_sample.shape[0]
    depth = g_ffn1.shape[0]
    wbuf = cache_attn_k.shape[2]
    assert x_sample.shape[1] == 1 and wbuf == DILATIONS[-1] * SPAN and nseq == 128
    tm_prompt, tm_sample, tc = 512, nseq, 512

    head_of_lane = np.arange(D_A) // HEAD_DIM
    bd = jnp.asarray((head_of_lane[:, None] == head_of_lane[None, :]) / HEAD_DIM, BF16)
    bias_p = _bias_prompt_call(rel_bias).reshape(
        len(DILATIONS), N_HEAD_PAIRS, HEADS_PER_VREG * BLK, 2 * BLK)
    s_bucket, s_count = _sample_tables(wbuf)
    bias_s = _bias_sample_call(rel_bias, jnp.asarray(s_bucket))
    s_count = jnp.asarray(s_count)
    rb0 = rel_bias[0].reshape(N_HEADS, 1)
    ck = jnp.transpose(cache_attn_k, (0, 1, 3, 4, 2))
    cv = jnp.transpose(cache_attn_v, (0, 1, 3, 4, 2))
    cct = jnp.transpose(cache_conv, (0, 2, 1, 3))
    cpt = jnp.transpose(cache_pool, (0, 2, 1, 3))

    hp = x_prompt.reshape(bsz * seq, D_MODEL)
    hs = x_sample.reshape(nseq, D_MODEL)
    outs = {n: [] for n in ("pk", "pv", "pc", "pp", "sk", "sv", "sc", "sp")}
    for l in range(depth):
        r1 = lambda a: a[l].reshape(1, -1)
        wgu1, wdn1 = w_ffn1_gu[l].astype(BF16), w_ffn1_down[l].astype(BF16)
        wgu2, wdn2 = w_ffn2_gu[l].astype(BF16), w_ffn2_down[l].astype(BF16)
        win, wout = w_in[l].astype(BF16), w_out[l].astype(BF16)
        gq = jnp.tile(g_q[l], N_HEADS).reshape(1, D_A)
        gk = jnp.tile(g_k[l], N_HEADS).reshape(1, D_A)
        pw = conv_pw[l].astype(BF16)
        plw = _block_diag(pool_w[l]).astype(BF16)
        mixw = (conv_w[l], r1(conv_b), r1(conv_ln_g), r1(conv_ln_b), pw, plw, r1(pool_scale))
        pre_w = (r1(g_ffn1), wgu1, wdn1, r1(g_mix), win, gq, gk, bd)
        post_w = (wout, r1(g_ffn2), wgu2, wdn2)

        hp, q, k, v, glu, c = _pre_call(hp, *pre_w, tm=tm_prompt)
        sq = lambda a: a.reshape(bsz, seq, a.shape[-1])
        attn = _attn_prompt_call(sq(q), sq(k), sq(v), bias_p)
        conv_o, pool_o = _mix_prompt_call(sq(glu), sq(c), *mixw, tc=tc)
        hp = _post_call(hp, attn.reshape(bsz * seq, D_A), conv_o.reshape(bsz * seq, D_B),
                        pool_o.reshape(bsz * seq, D_C), *post_w, tm=tm_prompt)
        keep = min(wbuf, seq)
        outs["pk"].append(sq(k)[:, seq - keep:].reshape(bsz, keep, N_HEADS, HEAD_DIM))
        outs["pv"].append(sq(v)[:, seq - keep:].reshape(bsz, keep, N_HEADS, HEAD_DIM))
        outs["pc"].append(sq(glu)[:, seq - (CONV_WIDTH - 1):])
        outs["pp"].append(sq(c)[:, seq - POOL_PREFIX:])

        hs, q, k, v, glu, c = _pre_call(hs, *pre_w, tm=tm_sample)
        attn_t = _attn_sample_call(l, q.T, k.T, v.T, ck, cv, bias_s, s_count, rb0)
        conv_o, pool_o = _mix_sample_call(l, glu, c, cct, cpt, *mixw)
        hs = _post_call(hs, attn_t.T, conv_o, pool_o, *post_w, tm=tm_sample)
        outs["sk"].append(k.reshape(nseq, 1, N_HEADS, HEAD_DIM))
        outs["sv"].append(v.reshape(nseq, 1, N_HEADS, HEAD_DIM))
        outs["sc"].append(jnp.concatenate([cache_conv[l][:, 1:], glu[:, None, :]], axis=1))
        outs["sp"].append(jnp.concatenate([cache_pool[l][:, 1:], c[:, None, :]], axis=1))

    st = lambda n: jnp.stack(outs[n])
    return (hp.reshape(bsz, seq, D_MODEL), hs.reshape(nseq, 1, D_MODEL),
            st("pk"), st("pv"), st("pc"), st("pp"), st("sk"), st("sv"), st("sc"), st("sp"))
```

```python
import functools
import math

import numpy as np
import jax
import jax.numpy as jnp
from jax import lax
from jax.experimental import pallas as pl
from jax.experimental.pallas import tpu as pltpu

F32 = jnp.float32
BF16 = jnp.bfloat16

D_MODEL = 1024
HEAD_DIM = 64
N_HEADS = 8
D_A = N_HEADS * HEAD_DIM
D_B = 256
D_C = 256
D_IN = 3 * D_A + 2 * D_B + D_C
D_FF = 2816
DILATIONS = (1, 4, 16)
SPAN = 128
BLK = 128
CONV_WIDTH = 31
POOL_WINDOWS = (2, 4, 8, 16)
POOL_GROUP = 64
POOL_PREFIX = 15
NUM_BUCKETS = 32
MAX_EXACT = 16
REL_MAX_DIST = 2048
EPS = 1e-6
NEG_INF = -1e30

HEADS_PER_VREG = 128 // HEAD_DIM
N_HEAD_PAIRS = N_HEADS // HEADS_PER_VREG
HALO = 32
FF_CHUNK = 256
ATTN_GROUP = 4
LOG2E = math.log2(math.e)
VMEM_LIMIT = 56 * 1024 * 1024


def _bucket_np(dist):
    dist = np.asarray(dist, np.int64)
    ratio = np.log(np.maximum(dist, 1).astype(np.float32) / np.float32(MAX_EXACT))
    large = MAX_EXACT + (ratio / np.float32(math.log(REL_MAX_DIST / MAX_EXACT))
                         * np.float32(NUM_BUCKETS - MAX_EXACT)).astype(np.int32)
    return np.where(dist < MAX_EXACT, dist, np.minimum(large, NUM_BUCKETS - 1)).astype(np.int32)


def _prompt_bucket_table():
    qi = np.arange(BLK)[:, None]
    kk = np.arange(2 * BLK)[None, :]
    dist = BLK + qi - kk
    ok = (dist >= 0) & (dist <= SPAN)
    tabs = []
    for dil in DILATIONS:
        b = _bucket_np(dil * np.clip(dist, 0, SPAN))
        tabs.append(np.where(ok, b, -1))
    return np.stack(tabs).astype(np.int32)


def _sample_tables(wbuf):
    dist = wbuf - np.arange(wbuf)
    count = np.zeros(wbuf, np.int32)
    for dil in DILATIONS:
        count += ((dist % dil == 0) & (dist // dil <= SPAN)).astype(np.int32)
    bucket = np.where(count > 0, _bucket_np(dist), -1).astype(np.int32)
    return bucket[None, :], count.astype(np.float32)[None, :]


def _dot(a, b):
    return jnp.dot(a, b, preferred_element_type=F32)


def _rms(x, g):
    ms = jnp.mean(x * x, axis=-1, keepdims=True)
    return x * lax.rsqrt(ms + EPS) * g


def _silu(x):
    return x * jax.nn.sigmoid(x)


def _swiglu_residual(h, g_ref, wgu_ref, wdn_ref):
    xn = _rms(h, g_ref[...]).astype(BF16)
    acc = None
    for c in range(D_FF // FF_CHUNK):
        lo = c * FF_CHUNK
        g = _dot(xn, wgu_ref[:, lo:lo + FF_CHUNK])
        u = _dot(xn, wgu_ref[:, D_FF + lo:D_FF + lo + FF_CHUNK])
        a = (_silu(g) * u).astype(BF16)
        d = _dot(a, wdn_ref[lo:lo + FF_CHUNK, :])
        acc = d if acc is None else acc + d
    return h + 0.5 * acc


def _head_rms(x, bd_ref, g):
    ms = _dot((x * x).astype(BF16), bd_ref[...])
    return x * lax.rsqrt(ms + EPS) * g


def _layernorm(x, g, b):
    mu = jnp.mean(x, axis=-1, keepdims=True)
    xc = x - mu
    var = jnp.mean(xc * xc, axis=-1, keepdims=True)
    return xc * lax.rsqrt(var + EPS) * g + b


def _pre_kernel(h_ref, gf_ref, wgu_ref, wdn_ref, gm_ref, win_ref, gq_ref, gk_ref, bd_ref,
                h1_ref, q_ref, k_ref, v_ref, glu_ref, c_ref, *t_refs):
    h1 = _swiglu_residual(h_ref[...], gf_ref, wgu_ref, wdn_ref)
    h1_ref[...] = h1
    xn = _rms(h1, gm_ref[...]).astype(BF16)
    q = _head_rms(_dot(xn, win_ref[:, 0:D_A]), bd_ref, gq_ref[...]) * (HEAD_DIM ** -0.5)
    k = _head_rms(_dot(xn, win_ref[:, D_A:2 * D_A]), bd_ref, gk_ref[...])
    v = _dot(xn, win_ref[:, 2 * D_A:3 * D_A])
    q_ref[...] = q
    k_ref[...] = k
    v_ref[...] = v
    for t_ref, val in zip(t_refs, (q, k, v)[3 - len(t_refs):]):
        t_ref[...] = val.T.reshape(t_ref.shape)
    o = 3 * D_A
    b_val = _dot(xn, win_ref[:, o:o + D_B])
    b_gate = _dot(xn, win_ref[:, o + D_B:o + 2 * D_B])
    glu_ref[...] = b_val * jax.nn.sigmoid(b_gate)
    c_ref[...] = _dot(xn, win_ref[:, o + 2 * D_B:o + 2 * D_B + D_C])


def _const_spec(shape):
    nd = len(shape)
    return pl.BlockSpec(shape, lambda i, _nd=nd: (0,) * _nd, pipeline_mode=pl.Buffered(1))


def _pre_call(h, gf, wgu, wdn, gm, win, gq, gk, bd, tm, seq=None, keep=None):
    t = h.shape[0]
    row = lambda w: pl.BlockSpec((tm, w), lambda i: (i, 0))
    out_specs = [row(D_MODEL), row(D_A), row(D_A), row(D_A), row(D_B), row(D_C)]
    out_shape = [jax.ShapeDtypeStruct((t, w), F32) for w in (D_MODEL, D_A, D_A, D_A, D_B, D_C)]
    if seq is not None:
        per_seq, skip = seq // tm, (seq - keep) // tm
        t_spec = pl.BlockSpec((1, D_A, tm),
                              lambda i: (i // per_seq, 0, jnp.maximum(i % per_seq - skip, 0)))
        out_specs += [t_spec] * 2
        out_shape += [jax.ShapeDtypeStruct((t // seq, D_A, keep), F32)] * 2
    else:
        out_specs += [pl.BlockSpec((D_A, tm), lambda i: (0, i))] * 3
        out_shape += [jax.ShapeDtypeStruct((D_A, t), F32)] * 3
    return pl.pallas_call(
        _pre_kernel,
        grid=(t // tm,),
        in_specs=[row(D_MODEL), _const_spec((1, D_MODEL)), _const_spec((D_MODEL, 2 * D_FF)),
                  _const_spec((D_FF, D_MODEL)), _const_spec((1, D_MODEL)),
                  _const_spec((D_MODEL, D_IN)), _const_spec((1, D_A)), _const_spec((1, D_A)),
                  _const_spec((D_A, D_A))],
        out_specs=out_specs,
        out_shape=out_shape,
        compiler_params=pltpu.CompilerParams(dimension_semantics=("arbitrary",),
                                             vmem_limit_bytes=VMEM_LIMIT),
        name="pre_ffn_inproj",
    )(h, gf, wgu, wdn, gm, win, gq, gk, bd)


def _pool_lane_consts(shape):
    grp = lax.broadcasted_iota(jnp.int32, shape, len(shape) - 1) // POOL_GROUP
    win = jnp.full(shape, POOL_WINDOWS[0], jnp.int32)
    for g in range(1, len(POOL_WINDOWS)):
        win = jnp.where(grp == g, POOL_WINDOWS[g], win)
    return win


def _conv_tail(y, cb_ref, lng_ref, lnb_ref, pw_ref):
    y = _layernorm(y + cb_ref[...], lng_ref[...], lnb_ref[...])
    return _dot(_silu(y).astype(BF16), pw_ref[...])


def _pool_tail(tot, x, cnt, plw_ref, psc_ref):
    d = tot / cnt - x
    return _dot(d.astype(BF16), plw_ref[...]) * psc_ref[...]


def _mix_ffn(h, attn, conv, pool, wout_ref, gf_ref, wgu_ref, wdn_ref):
    mix = _dot(attn.astype(BF16), wout_ref[0:D_A, :])
    mix = mix + _dot(conv.astype(BF16), wout_ref[D_A:D_A + D_B, :])
    mix = mix + _dot(pool.astype(BF16), wout_ref[D_A + D_B:D_A + D_B + D_C, :])
    return _swiglu_residual(h + mix, gf_ref, wgu_ref, wdn_ref)


def _post_prompt_kernel(h_ref, attn_ref, g_ref, gh_ref, c_ref, ch_ref,
                        cw_ref, cb_ref, lng_ref, lnb_ref, pw_ref, plw_ref, psc_ref,
                        wout_ref, gf_ref, wgu_ref, wdn_ref, out_ref, gx, cx, *, tm, per_seq):
    i = pl.program_id(0)
    tile = i % per_seq
    keep = (tile > 0).astype(F32)
    gx[0:HALO, :] = gh_ref[...] * keep
    gx[HALO:HALO + tm, :] = g_ref[...]
    cx[0:HALO, :] = ch_ref[...] * keep
    cx[HALO:HALO + tm, :] = c_ref[...]

    ext = tm + 8
    y = None
    for b in range(8):
        part = None
        for a in range(-(-CONV_WIDTH // 8)):
            lag = 8 * a + b
            if lag >= CONV_WIDTH:
                continue
            lo = HALO - 8 - 8 * a
            term = cw_ref[CONV_WIDTH - 1 - lag:CONV_WIDTH - lag, :] * gx[lo:lo + ext, :]
            part = term if part is None else part + term
        if b:
            part = pltpu.roll(part, b, 0)
        y = part if y is None else y + part
    conv = _conv_tail(y[8:], cb_ref, lng_ref, lnb_ref, pw_ref)

    x = c_ref[...]
    win = _pool_lane_consts((tm, D_C))
    run = cx[...]
    tot = jnp.zeros_like(x)
    prev_w = 1
    for w in POOL_WINDOWS:
        assert w == 2 * prev_w
        run = run + pltpu.roll(run, prev_w, 0)
        tot = jnp.where(win == w, run[HALO:], tot)
        prev_w = w
    pos = tile * tm + lax.broadcasted_iota(jnp.int32, (tm, D_C), 0)
    cnt = jnp.minimum(win, pos + 1).astype(F32)
    pool = _pool_tail(tot, x, cnt, plw_ref, psc_ref)

    out_ref[...] = _mix_ffn(h_ref[...], attn_ref[...], conv, pool, wout_ref, gf_ref, wgu_ref, wdn_ref)


def _post_sample_kernel(h_ref, attn_ref, g_ref, c_ref, cc_ref, cp_ref,
                        cw_ref, cb_ref, lng_ref, lnb_ref, pw_ref, plw_ref, psc_ref,
                        wout_ref, gf_ref, wgu_ref, wdn_ref, out_ref):
    n_hist = CONV_WIDTH - 1
    y = g_ref[...] * cw_ref[n_hist:n_hist + 1, :]
    for j in range(n_hist):
        y = y + cc_ref[0, j] * cw_ref[j:j + 1, :]
    conv = _conv_tail(y, cb_ref, lng_ref, lnb_ref, pw_ref)

    x = c_ref[...]
    win = _pool_lane_consts(x.shape)
    run = x
    tot = jnp.zeros_like(x)
    prev_w = 1
    for w in POOL_WINDOWS:
        for sft in range(prev_w, w):
            run = run + cp_ref[0, POOL_PREFIX - sft]
        tot = jnp.where(win == w, run, tot)
        prev_w = w
    pool = _pool_tail(tot, x, win.astype(F32), plw_ref, psc_ref)

    out_ref[...] = _mix_ffn(h_ref[...], attn_ref[...].T, conv, pool, wout_ref, gf_ref, wgu_ref, wdn_ref)


def _mixer_weight_specs():
    return [_const_spec((CONV_WIDTH, D_B)), _const_spec((1, D_B)), _const_spec((1, D_B)),
            _const_spec((1, D_B)), _const_spec((D_B, D_B)), _const_spec((D_C, D_C)),
            _const_spec((1, D_C)),
            _const_spec((D_MODEL, D_MODEL)), _const_spec((1, D_MODEL)),
            _const_spec((D_MODEL, 2 * D_FF)), _const_spec((D_FF, D_MODEL))]


def _post_prompt_call(h, attn, glu, c, mixw, postw, tm, seq):
    t = h.shape[0]
    per = tm // HALO
    row = lambda w: pl.BlockSpec((tm, w), lambda i: (i, 0))
    halo = lambda w: pl.BlockSpec((HALO, w), lambda i: (jnp.maximum(i * per - 1, 0), 0))
    return pl.pallas_call(
        functools.partial(_post_prompt_kernel, tm=tm, per_seq=seq // tm),
        grid=(t // tm,),
        in_specs=[row(D_MODEL), row(D_A), row(D_B), halo(D_B), row(D_C), halo(D_C)]
        + _mixer_weight_specs(),
        out_specs=row(D_MODEL),
        out_shape=jax.ShapeDtypeStruct((t, D_MODEL), F32),
        scratch_shapes=[pltpu.VMEM((HALO + tm, D_B), F32), pltpu.VMEM((HALO + tm, D_C), F32)],
        compiler_params=pltpu.CompilerParams(dimension_semantics=("arbitrary",),
                                             vmem_limit_bytes=VMEM_LIMIT),
        name="post_prompt",
    )(h, attn, glu, glu, c, c, *mixw, *postw)


def _post_sample_call(layer, h, attn, glu, c, cct, cpt, mixw, postw):
    t = h.shape[0]
    full = lambda w: pl.BlockSpec((t, w), lambda i: (0, 0))
    return pl.pallas_call(
        _post_sample_kernel,
        grid=(1,),
        in_specs=[full(D_MODEL), pl.BlockSpec((D_A, t), lambda i: (0, 0)), full(D_B), full(D_C),
                  pl.BlockSpec((1, CONV_WIDTH - 1, t, D_B), lambda i: (layer, 0, 0, 0)),
                  pl.BlockSpec((1, POOL_PREFIX, t, D_C), lambda i: (layer, 0, 0, 0))]
        + _mixer_weight_specs(),
        out_specs=full(D_MODEL),
        out_shape=jax.ShapeDtypeStruct((t, D_MODEL), F32),
        compiler_params=pltpu.CompilerParams(dimension_semantics=("arbitrary",),
                                             vmem_limit_bytes=VMEM_LIMIT),
        name="post_sample",
    )(h, attn, glu, c, cct, cpt, *mixw, *postw)


def _bias_prompt_kernel(rb_ref, tbl_ref, out_ref):
    h = pl.program_id(1)
    tbl = tbl_ref[0]

    def body(b, acc):
        return jnp.where(tbl == b, rb_ref[b, h], acc)

    acc = lax.fori_loop(0, NUM_BUCKETS, body, jnp.zeros(tbl.shape, F32))
    out_ref[0, 0] = jnp.where(tbl < 0, NEG_INF, acc * LOG2E)


def _bias_prompt_call(rel_bias):
    tbl = jnp.asarray(_prompt_bucket_table())
    nbr = len(DILATIONS)
    return pl.pallas_call(
        _bias_prompt_kernel,
        grid=(nbr, N_HEADS),
        in_specs=[pl.BlockSpec(memory_space=pltpu.SMEM),
                  pl.BlockSpec((1, BLK, 2 * BLK), lambda br, h: (br, 0, 0))],
        out_specs=pl.BlockSpec((1, 1, BLK, 2 * BLK), lambda br, h: (br, h, 0, 0)),
        out_shape=jax.ShapeDtypeStruct((nbr, N_HEADS, BLK, 2 * BLK), F32),
        name="bias_prompt",
    )(rel_bias, tbl)


def _bias_sample_kernel(rb_ref, tbl_ref, out_ref):
    tbl = tbl_ref[...]
    for h in range(N_HEADS):
        def body(b, acc, h=h):
            return jnp.where(tbl == b, rb_ref[b, h], acc)

        acc = lax.fori_loop(0, NUM_BUCKETS, body, jnp.zeros(tbl.shape, F32))
        out_ref[h:h + 1, :] = jnp.where(tbl < 0, NEG_INF, acc)


def _bias_sample_call(rel_bias, bucket):
    return pl.pallas_call(
        _bias_sample_kernel,
        in_specs=[pl.BlockSpec(memory_space=pltpu.SMEM),
                  pl.BlockSpec(bucket.shape, lambda: (0, 0))],
        out_shape=jax.ShapeDtypeStruct((N_HEADS, bucket.shape[1]), F32),
        name="bias_sample",
    )(rel_bias, bucket)


def _rows(start, size, dil):
    return pl.ds(start, size) if dil == 1 else pl.ds(start, size, stride=dil)


def _attn_prompt_kernel(q_ref, k_ref, v_ref, bias_ref, o_ref, m_acc, s_acc, n_acc, *, seq):
    lane_lo = lax.broadcasted_iota(jnp.int32, (BLK, 128), 1) < HEAD_DIM

    def group(br, dil, starts, n_keys, last):
        bias = bias_ref[br, 0, :, 2 * BLK - n_keys:]
        logits, values = [], []
        for q_start, k_start in starts:
            q = q_ref[0, _rows(q_start, BLK, dil), :] * LOG2E
            qs = jnp.concatenate([jnp.where(lane_lo, q, 0.0), jnp.where(lane_lo, 0.0, q)], axis=0)
            kb = k_ref[0, _rows(k_start, n_keys, dil), :].astype(BF16)
            values.append(v_ref[0, _rows(k_start, n_keys, dil), :].astype(BF16))
            logits.append(lax.dot_general(qs.astype(BF16), kb, (((1,), (1,)), ((), ())),
                                          preferred_element_type=F32) + bias)
        stats = []
        for lg, vb in zip(logits, values):
            m = jnp.max(lg, axis=-1, keepdims=True)
            p = jnp.exp2(lg - m)
            s = jnp.sum(p, axis=-1, keepdims=True)
            pv = _dot(p.astype(BF16), vb)
            stats.append((jnp.where(lane_lo, m[:BLK], m[BLK:]),
                          jnp.where(lane_lo, s[:BLK], s[BLK:]),
                          jnp.where(lane_lo, pv[:BLK], pv[BLK:])))
        rows = [_rows(q_start, BLK, dil) for q_start, _ in starts]
        if br == 0:
            for r, (mm, ss, num) in zip(rows, stats):
                m_acc[r, :] = mm
                s_acc[r, :] = ss
                n_acc[r, :] = num
            return
        old = [(m_acc[r, :], s_acc[r, :], n_acc[r, :]) for r in rows]
        for r, (mm, ss, num), (m_old, s_old, n_old) in zip(rows, stats, old):
            m_new = jnp.maximum(m_old, mm)
            a = jnp.exp2(m_old - m_new)
            b = jnp.exp2(mm - m_new)
            s_new = a * s_old + b * ss
            n_new = a * n_old + b * num
            if last:
                o_ref[0, r, :] = n_new / s_new
            else:
                m_acc[r, :] = m_new
                s_acc[r, :] = s_new
                n_acc[r, :] = n_new

    for br, dil in enumerate(DILATIONS):
        last = br == len(DILATIONS) - 1
        nb = seq // dil // BLK

        def starts_of(u, first, dil=dil):
            if first:
                return u, u
            lb = u // dil + 1
            r = u % dil
            return r + dil * BLK * lb, r + dil * BLK * (lb - 1)

        for first, n_units in ((True, dil), (False, dil * (nb - 1))):
            n_keys = BLK if first else 2 * BLK
            rem = n_units % ATTN_GROUP
            if rem:
                group(br, dil, [starts_of(u, first) for u in range(rem)], n_keys, last)

            def body(g, carry, br=br, dil=dil, last=last, first=first, rem=rem, n_keys=n_keys,
                     starts_of=starts_of):
                u0 = rem + g * ATTN_GROUP
                group(br, dil, [starts_of(u0 + j, first) for j in range(ATTN_GROUP)], n_keys, last)
                return carry

            lax.fori_loop(0, n_units // ATTN_GROUP, body, 0)


def _attn_prompt_call(q, k, v, bias2):
    bsz, seq, _ = q.shape
    blk = pl.BlockSpec((1, seq, 128), lambda b, hp: (b, 0, hp))
    return pl.pallas_call(
        functools.partial(_attn_prompt_kernel, seq=seq),
        grid=(bsz, N_HEAD_PAIRS),
        in_specs=[blk, blk, blk,
                  pl.BlockSpec((len(DILATIONS), 1, HEADS_PER_VREG * BLK, 2 * BLK),
                               lambda b, hp: (0, hp, 0, 0))],
        out_specs=blk,
        out_shape=jax.ShapeDtypeStruct((bsz, seq, D_A), F32),
        scratch_shapes=[pltpu.VMEM((seq, 128), F32)] * 3,
        compiler_params=pltpu.CompilerParams(dimension_semantics=("arbitrary", "arbitrary"),
                                             vmem_limit_bytes=VMEM_LIMIT),
        name="attn_prompt",
    )(q, k, v, bias2)


def _attn_sample_kernel(qt_ref, kt_ref, vt_ref, ck_ref, cv_ref, sbias_ref, cnt_ref, rb0_ref, o_ref):
    i = pl.program_id(0)
    sel = lax.broadcasted_iota(jnp.int32, (D_A, 128), 1) == i

    def column(ref):
        return jnp.sum(jnp.where(sel, ref[...], 0.0), axis=1, keepdims=True)

    qc, kc, vc = column(qt_ref), column(kt_ref), column(vt_ref)
    head = lambda a, h: a[h * HEAD_DIM:(h + 1) * HEAD_DIM]
    rows, new = [], []
    for h in range(N_HEADS):
        rows.append(jnp.sum(ck_ref[0, 0, h] * head(qc, h), axis=0, keepdims=True))
        new.append(jnp.sum(head(qc, h) * head(kc, h), axis=0, keepdims=True))
    lg = jnp.concatenate(rows, axis=0) + sbias_ref[...]
    lg0 = jnp.concatenate(new, axis=0) + rb0_ref[...]
    m = jnp.maximum(jnp.max(lg, axis=1, keepdims=True), lg0)
    p = jnp.exp(lg - m) * cnt_ref[...]
    p0 = len(DILATIONS) * jnp.exp(lg0 - m)
    den = jnp.sum(p, axis=1, keepdims=True) + p0
    cols = []
    for h in range(N_HEADS):
        num = jnp.sum(cv_ref[0, 0, h] * p[h:h + 1, :], axis=1, keepdims=True)
        num = num + p0[h:h + 1] * head(vc, h)
        cols.append(num / den[h:h + 1])
    out_col = jnp.concatenate(cols, axis=0)

    @pl.when(i == 0)
    def _():
        o_ref[...] = jnp.zeros(o_ref.shape, F32)

    o_ref[...] = jnp.where(sel, out_col, o_ref[...])


def _attn_sample_call(layer, qt, kt, vt, ck, cv, sbias, cnt, rb0):
    nseq = qt.shape[1]
    wbuf = ck.shape[-1]
    cst = lambda shape: pl.BlockSpec(shape, lambda i: (0,) * len(shape))
    cache = pl.BlockSpec((1, 1, N_HEADS, HEAD_DIM, wbuf), lambda i: (layer, i, 0, 0, 0))
    return pl.pallas_call(
        _attn_sample_kernel,
        grid=(nseq,),
        in_specs=[cst((D_A, nseq))] * 3 + [cache, cache, cst((N_HEADS, wbuf)), cst((1, wbuf)),
                                           cst((N_HEADS, 1))],
        out_specs=cst((D_A, nseq)),
        out_shape=jax.ShapeDtypeStruct((D_A, nseq), F32),
        compiler_params=pltpu.CompilerParams(dimension_semantics=("arbitrary",),
                                             vmem_limit_bytes=VMEM_LIMIT),
        name="attn_sample",
    )(qt, kt, vt, ck, cv, sbias, cnt, rb0)


def _block_diag(blocks):
    n, r, c = blocks.shape
    eye = jnp.eye(n, dtype=blocks.dtype)
    return (eye[:, None, :, None] * blocks[:, :, None, :]).reshape(n * r, n * c)


def kernel(x_prompt, x_sample, cache_attn_k, cache_attn_v, cache_conv, cache_pool, rel_bias, g_ffn1, w_ffn1_gu, w_ffn1_down, g_mix, w_in, g_q, g_k, conv_w, conv_b, conv_ln_g, conv_ln_b, conv_pw, pool_w, pool_scale, w_out, g_ffn2, w_ffn2_gu, w_ffn2_down):
    bsz, seq, _ = x_prompt.shape
    nseq = x_sample.shape[0]
    depth = g_ffn1.shape[0]
    wbuf = cache_attn_k.shape[2]
    assert x_sample.shape[1] == 1 and wbuf == DILATIONS[-1] * SPAN and nseq == 128
    tm_prompt, tm_sample = 512, nseq

    head_of_lane = np.arange(D_A) // HEAD_DIM
    bd = jnp.asarray((head_of_lane[:, None] == head_of_lane[None, :]) / HEAD_DIM, BF16)
    bias_p = _bias_prompt_call(rel_bias).reshape(
        len(DILATIONS), N_HEAD_PAIRS, HEADS_PER_VREG * BLK, 2 * BLK)
    s_bucket, s_count = _sample_tables(wbuf)
    bias_s = _bias_sample_call(rel_bias, jnp.asarray(s_bucket))
    s_count = jnp.asarray(s_count)
    rb0 = rel_bias[0].reshape(N_HEADS, 1)
    ck = jnp.transpose(cache_attn_k, (0, 1, 3, 4, 2))
    cv = jnp.transpose(cache_attn_v, (0, 1, 3, 4, 2))
    cct = jnp.transpose(cache_conv, (0, 2, 1, 3))
    cpt = jnp.transpose(cache_pool, (0, 2, 1, 3))

    hp = x_prompt.reshape(bsz * seq, D_MODEL)
    hs = x_sample.reshape(nseq, D_MODEL)
    outs = {n: [] for n in ("pk", "pv", "pc", "pp", "sk", "sv", "sc", "sp")}
    for l in range(depth):
        r1 = lambda a: a[l].reshape(1, -1)
        wgu1, wdn1 = w_ffn1_gu[l].astype(BF16), w_ffn1_down[l].astype(BF16)
        wgu2, wdn2 = w_ffn2_gu[l].astype(BF16), w_ffn2_down[l].astype(BF16)
        win, wout = w_in[l].astype(BF16), w_out[l].astype(BF16)
        gq = jnp.tile(g_q[l], N_HEADS).reshape(1, D_A)
        gk = jnp.tile(g_k[l], N_HEADS).reshape(1, D_A)
        pw = conv_pw[l].astype(BF16)
        plw = _block_diag(pool_w[l]).astype(BF16)
        mixw = (conv_w[l], r1(conv_b), r1(conv_ln_g), r1(conv_ln_b), pw, plw, r1(pool_scale))
        pre_w = (r1(g_ffn1), wgu1, wdn1, r1(g_mix), win, gq, gk, bd)
        post_w = (wout, r1(g_ffn2), wgu2, wdn2)

        keep = min(wbuf, seq)
        hp, q, k, v, glu, c, k_t, v_t = _pre_call(hp, *pre_w, tm=tm_prompt, seq=seq, keep=keep)
        sq = lambda a: a.reshape(bsz, seq, a.shape[-1])
        attn = _attn_prompt_call(sq(q), sq(k), sq(v), bias_p)
        hp = _post_prompt_call(hp, attn.reshape(bsz * seq, D_A), glu, c, mixw, post_w,
                               tm=tm_prompt, seq=seq)
        to_cache = lambda a: a.reshape(bsz, N_HEADS, HEAD_DIM, keep).transpose(0, 3, 1, 2)
        outs["pk"].append(to_cache(k_t))
        outs["pv"].append(to_cache(v_t))
        outs["pc"].append(sq(glu)[:, seq - (CONV_WIDTH - 1):])
        outs["pp"].append(sq(c)[:, seq - POOL_PREFIX:])

        hs, q, k, v, glu, c, q_t, k_t, v_t = _pre_call(hs, *pre_w, tm=tm_sample)
        attn_t = _attn_sample_call(l, q_t, k_t, v_t, ck, cv, bias_s, s_count, rb0)
        hs = _post_sample_call(l, hs, attn_t, glu, c, cct, cpt, mixw, post_w)
        outs["sk"].append(k.reshape(nseq, 1, N_HEADS, HEAD_DIM))
        outs["sv"].append(v.reshape(nseq, 1, N_HEADS, HEAD_DIM))
        outs["sc"].append(jnp.concatenate([cache_conv[l][:, 1:], glu[:, None, :]], axis=1))
        outs["sp"].append(jnp.concatenate([cache_pool[l][:, 1:], c[:, None, :]], axis=1))

    st = lambda n: jnp.stack(outs[n])
    return (hp.reshape(bsz, seq, D_MODEL), hs.reshape(nseq, 1, D_MODEL),
            st("pk"), st("pv"), st("pc"), st("pp"), st("sk"), st("sv"), st("sc"), st("sp"))
```

```python
import functools
import math

import numpy as np
import jax
import jax.numpy as jnp
from jax import lax
from jax.experimental import pallas as pl
from jax.experimental.pallas import tpu as pltpu

F32 = jnp.float32
BF16 = jnp.bfloat16

D_MODEL = 1024
HEAD_DIM = 64
N_HEADS = 8
D_A = N_HEADS * HEAD_DIM
D_B = 256
D_C = 256
D_IN = 3 * D_A + 2 * D_B + D_C
D_FF = 2816
DILATIONS = (1, 4, 16)
SPAN = 128
BLK = 128
CONV_WIDTH = 31
POOL_WINDOWS = (2, 4, 8, 16)
POOL_GROUP = 64
POOL_PREFIX = 15
NUM_BUCKETS = 32
MAX_EXACT = 16
REL_MAX_DIST = 2048
EPS = 1e-6
NEG_INF = -1e30

HEADS_PER_VREG = 128 // HEAD_DIM
N_HEAD_PAIRS = N_HEADS // HEADS_PER_VREG
HALO = 32
FF_CHUNK = 256
ATTN_GROUP = 4
SA_UNITS_PER_SEQ = 2
SA_SLOTS = 2
LOG2E = math.log2(math.e)
VMEM_LIMIT = 56 * 1024 * 1024


def _bucket_np(dist):
    dist = np.asarray(dist, np.int64)
    ratio = np.log(np.maximum(dist, 1).astype(np.float32) / np.float32(MAX_EXACT))
    large = MAX_EXACT + (ratio / np.float32(math.log(REL_MAX_DIST / MAX_EXACT))
                         * np.float32(NUM_BUCKETS - MAX_EXACT)).astype(np.int32)
    return np.where(dist < MAX_EXACT, dist, np.minimum(large, NUM_BUCKETS - 1)).astype(np.int32)


def _prompt_bucket_table():
    qi = np.arange(BLK)[:, None]
    kk = np.arange(2 * BLK)[None, :]
    dist = BLK + qi - kk
    ok = (dist >= 0) & (dist <= SPAN)
    tabs = []
    for dil in DILATIONS:
        b = _bucket_np(dil * np.clip(dist, 0, SPAN))
        tabs.append(np.where(ok, b, -1))
    return np.stack(tabs).astype(np.int32)


def _sample_tables(wbuf):
    dist = wbuf - np.arange(wbuf)
    count = np.zeros(wbuf, np.int32)
    for dil in DILATIONS:
        count += ((dist % dil == 0) & (dist // dil <= SPAN)).astype(np.int32)
    bucket = np.where(count > 0, _bucket_np(dist), -1).astype(np.int32)
    return bucket[None, :], count.astype(np.float32)[None, :]


def _dot(a, b):
    return jnp.dot(a, b, preferred_element_type=F32)


def _rms(x, g):
    ms = jnp.mean(x * x, axis=-1, keepdims=True)
    return x * lax.rsqrt(ms + EPS) * g


def _silu(x):
    return x * jax.nn.sigmoid(x)


def _swiglu_residual(h, g_ref, wgu_ref, wdn_ref, side_work=()):
    xn = _rms(h, g_ref[...]).astype(BF16)
    acc = None
    for c in range(D_FF // FF_CHUNK):
        lo = c * FF_CHUNK
        g = _dot(xn, wgu_ref[:, lo:lo + FF_CHUNK])
        u = _dot(xn, wgu_ref[:, D_FF + lo:D_FF + lo + FF_CHUNK])
        a = (_silu(g) * u).astype(BF16)
        d = _dot(a, wdn_ref[lo:lo + FF_CHUNK, :])
        acc = d if acc is None else acc + d
        if c < len(side_work):
            side_work[c]()
    assert len(side_work) <= D_FF // FF_CHUNK
    return h + 0.5 * acc


def _head_rms(x, bd_ref, g):
    ms = _dot((x * x).astype(BF16), bd_ref[...])
    return x * lax.rsqrt(ms + EPS) * g


def _layernorm(x, g, b):
    mu = jnp.mean(x, axis=-1, keepdims=True)
    xc = x - mu
    var = jnp.mean(xc * xc, axis=-1, keepdims=True)
    return xc * lax.rsqrt(var + EPS) * g + b


def _pre_kernel(h_ref, gf_ref, wgu_ref, wdn_ref, gm_ref, win_ref, gq_ref, gk_ref, bd_ref,
                h1_ref, q_ref, k_ref, v_ref, glu_ref, c_ref, *t_refs):
    h1 = _swiglu_residual(h_ref[...], gf_ref, wgu_ref, wdn_ref)
    h1_ref[...] = h1
    xn = _rms(h1, gm_ref[...]).astype(BF16)
    q = _head_rms(_dot(xn, win_ref[:, 0:D_A]), bd_ref, gq_ref[...]) * (HEAD_DIM ** -0.5)
    k = _head_rms(_dot(xn, win_ref[:, D_A:2 * D_A]), bd_ref, gk_ref[...])
    v = _dot(xn, win_ref[:, 2 * D_A:3 * D_A])
    q_ref[...] = q
    k_ref[...] = k
    v_ref[...] = v
    for t_ref, val in zip(t_refs, (q, k, v)[3 - len(t_refs):]):
        t_ref[...] = val.T.reshape(t_ref.shape)
    o = 3 * D_A
    b_val = _dot(xn, win_ref[:, o:o + D_B])
    b_gate = _dot(xn, win_ref[:, o + D_B:o + 2 * D_B])
    glu_ref[...] = b_val * jax.nn.sigmoid(b_gate)
    c_ref[...] = _dot(xn, win_ref[:, o + 2 * D_B:o + 2 * D_B + D_C])


def _const_spec(shape):
    nd = len(shape)
    return pl.BlockSpec(shape, lambda i, _nd=nd: (0,) * _nd, pipeline_mode=pl.Buffered(1))


def _pre_call(h, gf, wgu, wdn, gm, win, gq, gk, bd, tm, seq=None, keep=None):
    t = h.shape[0]
    row = lambda w: pl.BlockSpec((tm, w), lambda i: (i, 0))
    out_specs = [row(D_MODEL), row(D_A), row(D_A), row(D_A), row(D_B), row(D_C)]
    out_shape = [jax.ShapeDtypeStruct((t, w), F32) for w in (D_MODEL, D_A, D_A, D_A, D_B, D_C)]
    if seq is not None:
        per_seq, skip = seq // tm, (seq - keep) // tm
        t_spec = pl.BlockSpec((1, D_A, tm),
                              lambda i: (i // per_seq, 0, jnp.maximum(i % per_seq - skip, 0)))
        out_specs += [t_spec] * 2
        out_shape += [jax.ShapeDtypeStruct((t // seq, D_A, keep), F32)] * 2
    else:
        out_specs += [pl.BlockSpec((D_A, tm), lambda i: (0, i))] * 3
        out_shape += [jax.ShapeDtypeStruct((D_A, t), F32)] * 3
    return pl.pallas_call(
        _pre_kernel,
        grid=(t // tm,),
        in_specs=[row(D_MODEL), _const_spec((1, D_MODEL)), _const_spec((D_MODEL, 2 * D_FF)),
                  _const_spec((D_FF, D_MODEL)), _const_spec((1, D_MODEL)),
                  _const_spec((D_MODEL, D_IN)), _const_spec((1, D_A)), _const_spec((1, D_A)),
                  _const_spec((D_A, D_A))],
        out_specs=out_specs,
        out_shape=out_shape,
        compiler_params=pltpu.CompilerParams(dimension_semantics=("arbitrary",),
                                             vmem_limit_bytes=VMEM_LIMIT),
        name="pre_ffn_inproj",
    )(h, gf, wgu, wdn, gm, win, gq, gk, bd)


def _pool_lane_consts(shape):
    grp = lax.broadcasted_iota(jnp.int32, shape, len(shape) - 1) // POOL_GROUP
    win = jnp.full(shape, POOL_WINDOWS[0], jnp.int32)
    for g in range(1, len(POOL_WINDOWS)):
        win = jnp.where(grp == g, POOL_WINDOWS[g], win)
    return win


def _conv_tail(y, cb_ref, lng_ref, lnb_ref, pw_ref):
    y = _layernorm(y + cb_ref[...], lng_ref[...], lnb_ref[...])
    return _dot(_silu(y).astype(BF16), pw_ref[...])


def _pool_tail(tot, x, cnt, plw_ref, psc_ref):
    d = tot / cnt - x
    return _dot(d.astype(BF16), plw_ref[...]) * psc_ref[...]


def _mix_ffn(h, attn, conv, pool, wout_ref, gf_ref, wgu_ref, wdn_ref, side_work=()):
    mix = _dot(attn.astype(BF16), wout_ref[0:D_A, :])
    mix = mix + _dot(conv.astype(BF16), wout_ref[D_A:D_A + D_B, :])
    mix = mix + _dot(pool.astype(BF16), wout_ref[D_A + D_B:D_A + D_B + D_C, :])
    return _swiglu_residual(h + mix, gf_ref, wgu_ref, wdn_ref, side_work)


def _sample_attn_side_work(i, n_steps, layer, qt_ref, kt_ref, vt_ref, ck_hbm, cv_hbm,
                           sbias_ref, cnt_ref, rb0_ref, o_ref, kbuf, vbuf, sems):
    n_seq = qt_ref.shape[1]
    units = SA_UNITS_PER_SEQ * n_seq // n_steps
    assert units * n_steps == SA_UNITS_PER_SEQ * n_seq and units % SA_SLOTS == 0
    heads = N_HEADS // SA_UNITS_PER_SEQ

    def copies(step, j):
        s = step * (units // SA_UNITS_PER_SEQ) + j // SA_UNITS_PER_SEQ
        g, slot = j % SA_UNITS_PER_SEQ, j % SA_SLOTS
        return [pltpu.make_async_copy(hbm.at[layer, s, pl.ds(g * heads, heads)], buf.at[slot],
                                      sems.at[n, slot])
                for n, (hbm, buf) in enumerate(((ck_hbm, kbuf), (cv_hbm, vbuf)))]

    def start(step, j):
        for cp in copies(step, j):
            cp.start()

    @pl.when(i == 0)
    def _():
        o_ref[...] = jnp.zeros(o_ref.shape, F32)
        for j in range(SA_SLOTS):
            start(i, j)

    lane = lax.broadcasted_iota(jnp.int32, (D_A, n_seq), 1)
    head = lambda a, h: a[h * HEAD_DIM:(h + 1) * HEAD_DIM]
    cols = {}

    def unit(j):
        s = i * (units // SA_UNITS_PER_SEQ) + j // SA_UNITS_PER_SEQ
        g, slot = j % SA_UNITS_PER_SEQ, j % SA_SLOTS
        sel = lane == s
        if j // SA_UNITS_PER_SEQ not in cols:
            cols[j // SA_UNITS_PER_SEQ] = [
                jnp.sum(jnp.where(sel, r[...], 0.0), axis=1, keepdims=True)
                for r in (qt_ref, kt_ref, vt_ref)]
        qc, kc, vc = cols[j // SA_UNITS_PER_SEQ]
        for cp in copies(i, j):
            cp.wait()
        h0 = g * heads
        rows, new = [], []
        for hh in range(heads):
            rows.append(jnp.sum(kbuf[slot, hh] * head(qc, h0 + hh), axis=0, keepdims=True))
            new.append(jnp.sum(head(qc, h0 + hh) * head(kc, h0 + hh), axis=0, keepdims=True))
        lg = jnp.concatenate(rows, axis=0) + sbias_ref[h0:h0 + heads, :]
        lg0 = jnp.concatenate(new, axis=0) + rb0_ref[h0:h0 + heads, :]
        m = jnp.maximum(jnp.max(lg, axis=1, keepdims=True), lg0)
        p = jnp.exp(lg - m) * cnt_ref[...]
        p0 = len(DILATIONS) * jnp.exp(lg0 - m)
        den = jnp.sum(p, axis=1, keepdims=True) + p0
        outs = []
        for hh in range(heads):
            num = jnp.sum(vbuf[slot, hh] * p[hh:hh + 1, :], axis=1, keepdims=True)
            num = num + p0[hh:hh + 1] * head(vc, h0 + hh)
            outs.append(num / den[hh:hh + 1])
        r0, r1 = h0 * HEAD_DIM, (h0 + heads) * HEAD_DIM
        mine = lax.broadcasted_iota(jnp.int32, (r1 - r0, n_seq), 1) == s
        o_ref[r0:r1, :] = jnp.where(mine, jnp.concatenate(outs, axis=0), o_ref[r0:r1, :])
        nxt = j + SA_SLOTS
        if nxt < units:
            start(i, nxt)
        else:
            @pl.when(i + 1 < n_steps)
            def _():
                start(i + 1, nxt - units)

    return [functools.partial(unit, j) for j in range(units)]


def _post_prompt_kernel(h_ref, attn_ref, g_ref, gh_ref, c_ref, ch_ref,
                        cw_ref, cb_ref, lng_ref, lnb_ref, pw_ref, plw_ref, psc_ref,
                        wout_ref, gf_ref, wgu_ref, wdn_ref,
                        qt_ref, kt_ref, vt_ref, sbias_ref, cnt_ref, rb0_ref, ck_hbm, cv_hbm,
                        out_ref, sa_ref, gx, cx, kbuf, vbuf, sems, *, tm, per_seq, layer, n_steps):
    i = pl.program_id(0)
    side_work = _sample_attn_side_work(i, n_steps, layer, qt_ref, kt_ref, vt_ref, ck_hbm, cv_hbm,
                                       sbias_ref, cnt_ref, rb0_ref, sa_ref, kbuf, vbuf, sems)
    tile = i % per_seq
    keep = (tile > 0).astype(F32)
    gx[0:HALO, :] = gh_ref[...] * keep
    gx[HALO:HALO + tm, :] = g_ref[...]
    cx[0:HALO, :] = ch_ref[...] * keep
    cx[HALO:HALO + tm, :] = c_ref[...]

    ext = tm + 8
    y = None
    for b in range(8):
        part = None
        for a in range(-(-CONV_WIDTH // 8)):
            lag = 8 * a + b
            if lag >= CONV_WIDTH:
                continue
            lo = HALO - 8 - 8 * a
            term = cw_ref[CONV_WIDTH - 1 - lag:CONV_WIDTH - lag, :] * gx[lo:lo + ext, :]
            part = term if part is None else part + term
        if b:
            part = pltpu.roll(part, b, 0)
        y = part if y is None else y + part
    conv = _conv_tail(y[8:], cb_ref, lng_ref, lnb_ref, pw_ref)

    x = c_ref[...]
    win = _pool_lane_consts((tm, D_C))
    run = cx[...]
    tot = jnp.zeros_like(x)
    prev_w = 1
    for w in POOL_WINDOWS:
        assert w == 2 * prev_w
        run = run + pltpu.roll(run, prev_w, 0)
        tot = jnp.where(win == w, run[HALO:], tot)
        prev_w = w
    pos = tile * tm + lax.broadcasted_iota(jnp.int32, (tm, D_C), 0)
    cnt = jnp.minimum(win, pos + 1).astype(F32)
    pool = _pool_tail(tot, x, cnt, plw_ref, psc_ref)

    out_ref[...] = _mix_ffn(h_ref[...], attn_ref[...], conv, pool, wout_ref, gf_ref, wgu_ref, wdn_ref,
                            side_work)


def _post_sample_kernel(h_ref, attn_ref, g_ref, c_ref, cc_ref, cp_ref,
                        cw_ref, cb_ref, lng_ref, lnb_ref, pw_ref, plw_ref, psc_ref,
                        wout_ref, gf_ref, wgu_ref, wdn_ref, out_ref):
    n_hist = CONV_WIDTH - 1
    y = g_ref[...] * cw_ref[n_hist:n_hist + 1, :]
    for j in range(n_hist):
        y = y + cc_ref[0, j] * cw_ref[j:j + 1, :]
    conv = _conv_tail(y, cb_ref, lng_ref, lnb_ref, pw_ref)

    x = c_ref[...]
    win = _pool_lane_consts(x.shape)
    run = x
    tot = jnp.zeros_like(x)
    prev_w = 1
    for w in POOL_WINDOWS:
        for sft in range(prev_w, w):
            run = run + cp_ref[0, POOL_PREFIX - sft]
        tot = jnp.where(win == w, run, tot)
        prev_w = w
    pool = _pool_tail(tot, x, win.astype(F32), plw_ref, psc_ref)

    out_ref[...] = _mix_ffn(h_ref[...], attn_ref[...].T, conv, pool, wout_ref, gf_ref, wgu_ref, wdn_ref)


def _mixer_weight_specs():
    return [_const_spec((CONV_WIDTH, D_B)), _const_spec((1, D_B)), _const_spec((1, D_B)),
            _const_spec((1, D_B)), _const_spec((D_B, D_B)), _const_spec((D_C, D_C)),
            _const_spec((1, D_C)),
            _const_spec((D_MODEL, D_MODEL)), _const_spec((1, D_MODEL)),
            _const_spec((D_MODEL, 2 * D_FF)), _const_spec((D_FF, D_MODEL))]


def _post_prompt_call(layer, h, attn, glu, c, mixw, postw, sample, tm, seq):
    t = h.shape[0]
    per = tm // HALO
    n_steps = t // tm
    qt, ck = sample[0], sample[-1]
    n_seq, wbuf = qt.shape[1], ck.shape[-1]
    heads = N_HEADS // SA_UNITS_PER_SEQ
    row = lambda w: pl.BlockSpec((tm, w), lambda i: (i, 0))
    halo = lambda w: pl.BlockSpec((HALO, w), lambda i: (jnp.maximum(i * per - 1, 0), 0))
    cst = lambda shape: pl.BlockSpec(shape, lambda i: (0,) * len(shape))
    hbm = pl.BlockSpec(memory_space=pl.ANY)
    return pl.pallas_call(
        functools.partial(_post_prompt_kernel, tm=tm, per_seq=seq // tm, layer=layer,
                          n_steps=n_steps),
        grid=(n_steps,),
        in_specs=[row(D_MODEL), row(D_A), row(D_B), halo(D_B), row(D_C), halo(D_C)]
        + _mixer_weight_specs()
        + [cst((D_A, n_seq))] * 3 + [cst((N_HEADS, wbuf)), cst((1, wbuf)), cst((N_HEADS, 1)),
                                     hbm, hbm],
        out_specs=[row(D_MODEL), cst((D_A, n_seq))],
        out_shape=[jax.ShapeDtypeStruct((t, D_MODEL), F32),
                   jax.ShapeDtypeStruct((D_A, n_seq), F32)],
        scratch_shapes=[pltpu.VMEM((HALO + tm, D_B), F32), pltpu.VMEM((HALO + tm, D_C), F32),
                        pltpu.VMEM((SA_SLOTS, heads, HEAD_DIM, wbuf), F32),
                        pltpu.VMEM((SA_SLOTS, heads, HEAD_DIM, wbuf), F32),
                        pltpu.SemaphoreType.DMA((2, SA_SLOTS))],
        compiler_params=pltpu.CompilerParams(dimension_semantics=("arbitrary",),
                                             vmem_limit_bytes=VMEM_LIMIT),
        name="post_prompt",
    )(h, attn, glu, glu, c, c, *mixw, *postw, *sample)


def _post_sample_call(layer, h, attn, glu, c, cct, cpt, mixw, postw):
    t = h.shape[0]
    full = lambda w: pl.BlockSpec((t, w), lambda i: (0, 0))
    return pl.pallas_call(
        _post_sample_kernel,
        grid=(1,),
        in_specs=[full(D_MODEL), pl.BlockSpec((D_A, t), lambda i: (0, 0)), full(D_B), full(D_C),
                  pl.BlockSpec((1, CONV_WIDTH - 1, t, D_B), lambda i: (layer, 0, 0, 0)),
                  pl.BlockSpec((1, POOL_PREFIX, t, D_C), lambda i: (layer, 0, 0, 0))]
        + _mixer_weight_specs(),
        out_specs=full(D_MODEL),
        out_shape=jax.ShapeDtypeStruct((t, D_MODEL), F32),
        compiler_params=pltpu.CompilerParams(dimension_semantics=("arbitrary",),
                                             vmem_limit_bytes=VMEM_LIMIT),
        name="post_sample",
    )(h, attn, glu, c, cct, cpt, *mixw, *postw)


def _bias_prompt_kernel(rb_ref, tbl_ref, out_ref):
    h = pl.program_id(1)
    tbl = tbl_ref[0]

    def body(b, acc):
        return jnp.where(tbl == b, rb_ref[b, h], acc)

    acc = lax.fori_loop(0, NUM_BUCKETS, body, jnp.zeros(tbl.shape, F32))
    out_ref[0, 0] = jnp.where(tbl < 0, NEG_INF, acc * LOG2E)


def _bias_prompt_call(rel_bias):
    tbl = jnp.asarray(_prompt_bucket_table())
    nbr = len(DILATIONS)
    return pl.pallas_call(
        _bias_prompt_kernel,
        grid=(nbr, N_HEADS),
        in_specs=[pl.BlockSpec(memory_space=pltpu.SMEM),
                  pl.BlockSpec((1, BLK, 2 * BLK), lambda br, h: (br, 0, 0))],
        out_specs=pl.BlockSpec((1, 1, BLK, 2 * BLK), lambda br, h: (br, h, 0, 0)),
        out_shape=jax.ShapeDtypeStruct((nbr, N_HEADS, BLK, 2 * BLK), F32),
        name="bias_prompt",
    )(rel_bias, tbl)


def _bias_sample_kernel(rb_ref, tbl_ref, out_ref):
    tbl = tbl_ref[...]
    for h in range(N_HEADS):
        def body(b, acc, h=h):
            return jnp.where(tbl == b, rb_ref[b, h], acc)

        acc = lax.fori_loop(0, NUM_BUCKETS, body, jnp.zeros(tbl.shape, F32))
        out_ref[h:h + 1, :] = jnp.where(tbl < 0, NEG_INF, acc)


def _bias_sample_call(rel_bias, bucket):
    return pl.pallas_call(
        _bias_sample_kernel,
        in_specs=[pl.BlockSpec(memory_space=pltpu.SMEM),
                  pl.BlockSpec(bucket.shape, lambda: (0, 0))],
        out_shape=jax.ShapeDtypeStruct((N_HEADS, bucket.shape[1]), F32),
        name="bias_sample",
    )(rel_bias, bucket)


def _rows(start, size, dil):
    return pl.ds(start, size) if dil == 1 else pl.ds(start, size, stride=dil)


def _attn_prompt_kernel(q_ref, k_ref, v_ref, bias_ref, o_ref, m_acc, s_acc, n_acc, *, seq):
    lane_lo = lax.broadcasted_iota(jnp.int32, (BLK, 128), 1) < HEAD_DIM

    def group(br, dil, starts, n_keys, last):
        bias = bias_ref[br, 0, :, 2 * BLK - n_keys:]
        logits, values = [], []
        for q_start, k_start in starts:
            q = q_ref[0, _rows(q_start, BLK, dil), :] * LOG2E
            qs = jnp.concatenate([jnp.where(lane_lo, q, 0.0), jnp.where(lane_lo, 0.0, q)], axis=0)
            kb = k_ref[0, _rows(k_start, n_keys, dil), :].astype(BF16)
            values.append(v_ref[0, _rows(k_start, n_keys, dil), :].astype(BF16))
            logits.append(lax.dot_general(qs.astype(BF16), kb, (((1,), (1,)), ((), ())),
                                          preferred_element_type=F32) + bias)
        stats = []
        for lg, vb in zip(logits, values):
            m = jnp.max(lg, axis=-1, keepdims=True)
            p = jnp.exp2(lg - m)
            s = jnp.sum(p, axis=-1, keepdims=True)
            pv = _dot(p.astype(BF16), vb)
            stats.append((jnp.where(lane_lo, m[:BLK], m[BLK:]),
                          jnp.where(lane_lo, s[:BLK], s[BLK:]),
                          jnp.where(lane_lo, pv[:BLK], pv[BLK:])))
        rows = [_rows(q_start, BLK, dil) for q_start, _ in starts]
        if br == 0:
            for r, (mm, ss, num) in zip(rows, stats):
                m_acc[r, :] = mm
                s_acc[r, :] = ss
                n_acc[r, :] = num
            return
        old = [(m_acc[r, :], s_acc[r, :], n_acc[r, :]) for r in rows]
        for r, (mm, ss, num), (m_old, s_old, n_old) in zip(rows, stats, old):
            m_new = jnp.maximum(m_old, mm)
            a = jnp.exp2(m_old - m_new)
            b = jnp.exp2(mm - m_new)
            s_new = a * s_old + b * ss
            n_new = a * n_old + b * num
            if last:
                o_ref[0, r, :] = n_new / s_new
            else:
                m_acc[r, :] = m_new
                s_acc[r, :] = s_new
                n_acc[r, :] = n_new

    for br, dil in enumerate(DILATIONS):
        last = br == len(DILATIONS) - 1
        nb = seq // dil // BLK

        def starts_of(u, first, dil=dil):
            if first:
                return u, u
            lb = u // dil + 1
            r = u % dil
            return r + dil * BLK * lb, r + dil * BLK * (lb - 1)

        for first, n_units in ((True, dil), (False, dil * (nb - 1))):
            n_keys = BLK if first else 2 * BLK
            rem = n_units % ATTN_GROUP
            if rem:
                group(br, dil, [starts_of(u, first) for u in range(rem)], n_keys, last)

            def body(g, carry, br=br, dil=dil, last=last, first=first, rem=rem, n_keys=n_keys,
                     starts_of=starts_of):
                u0 = rem + g * ATTN_GROUP
                group(br, dil, [starts_of(u0 + j, first) for j in range(ATTN_GROUP)], n_keys, last)
                return carry

            lax.fori_loop(0, n_units // ATTN_GROUP, body, 0)


def _attn_prompt_call(q, k, v, bias2):
    bsz, seq, _ = q.shape
    blk = pl.BlockSpec((1, seq, 128), lambda b, hp: (b, 0, hp))
    return pl.pallas_call(
        functools.partial(_attn_prompt_kernel, seq=seq),
        grid=(bsz, N_HEAD_PAIRS),
        in_specs=[blk, blk, blk,
                  pl.BlockSpec((len(DILATIONS), 1, HEADS_PER_VREG * BLK, 2 * BLK),
                               lambda b, hp: (0, hp, 0, 0))],
        out_specs=blk,
        out_shape=jax.ShapeDtypeStruct((bsz, seq, D_A), F32),
        scratch_shapes=[pltpu.VMEM((seq, 128), F32)] * 3,
        compiler_params=pltpu.CompilerParams(dimension_semantics=("arbitrary", "arbitrary"),
                                             vmem_limit_bytes=VMEM_LIMIT),
        name="attn_prompt",
    )(q, k, v, bias2)


def _block_diag(blocks):
    n, r, c = blocks.shape
    eye = jnp.eye(n, dtype=blocks.dtype)
    return (eye[:, None, :, None] * blocks[:, :, None, :]).reshape(n * r, n * c)


def kernel(x_prompt, x_sample, cache_attn_k, cache_attn_v, cache_conv, cache_pool, rel_bias, g_ffn1, w_ffn1_gu, w_ffn1_down, g_mix, w_in, g_q, g_k, conv_w, conv_b, conv_ln_g, conv_ln_b, conv_pw, pool_w, pool_scale, w_out, g_ffn2, w_ffn2_gu, w_ffn2_down):
    bsz, seq, _ = x_prompt.shape
    nseq = x_sample.shape[0]
    depth = g_ffn1.shape[0]
    wbuf = cache_attn_k.shape[2]
    assert x_sample.shape[1] == 1 and wbuf == DILATIONS[-1] * SPAN and nseq == 128
    tm_prompt, tm_sample = 512, nseq

    head_of_lane = np.arange(D_A) // HEAD_DIM
    bd = jnp.asarray((head_of_lane[:, None] == head_of_lane[None, :]) / HEAD_DIM, BF16)
    bias_p = _bias_prompt_call(rel_bias).reshape(
        len(DILATIONS), N_HEAD_PAIRS, HEADS_PER_VREG * BLK, 2 * BLK)
    s_bucket, s_count = _sample_tables(wbuf)
    bias_s = _bias_sample_call(rel_bias, jnp.asarray(s_bucket))
    s_count = jnp.asarray(s_count)
    rb0 = rel_bias[0].reshape(N_HEADS, 1)
    ck = jnp.transpose(cache_attn_k, (0, 1, 3, 4, 2))
    cv = jnp.transpose(cache_attn_v, (0, 1, 3, 4, 2))
    cct = jnp.transpose(cache_conv, (0, 2, 1, 3))
    cpt = jnp.transpose(cache_pool, (0, 2, 1, 3))

    hp = x_prompt.reshape(bsz * seq, D_MODEL)
    hs = x_sample.reshape(nseq, D_MODEL)
    outs = {n: [] for n in ("pk", "pv", "pc", "pp", "sk", "sv", "sc", "sp")}
    for l in range(depth):
        r1 = lambda a: a[l].reshape(1, -1)
        wgu1, wdn1 = w_ffn1_gu[l].astype(BF16), w_ffn1_down[l].astype(BF16)
        wgu2, wdn2 = w_ffn2_gu[l].astype(BF16), w_ffn2_down[l].astype(BF16)
        win, wout = w_in[l].astype(BF16), w_out[l].astype(BF16)
        gq = jnp.tile(g_q[l], N_HEADS).reshape(1, D_A)
        gk = jnp.tile(g_k[l], N_HEADS).reshape(1, D_A)
        pw = conv_pw[l].astype(BF16)
        plw = _block_diag(pool_w[l]).astype(BF16)
        mixw = (conv_w[l], r1(conv_b), r1(conv_ln_g), r1(conv_ln_b), pw, plw, r1(pool_scale))
        pre_w = (r1(g_ffn1), wgu1, wdn1, r1(g_mix), win, gq, gk, bd)
        post_w = (wout, r1(g_ffn2), wgu2, wdn2)

        keep = min(wbuf, seq)
        hp, q, k, v, glu, c, k_t, v_t = _pre_call(hp, *pre_w, tm=tm_prompt, seq=seq, keep=keep)
        hs, _, sk, sv, sglu, sc, sq_t, sk_t, sv_t = _pre_call(hs, *pre_w, tm=tm_sample)

        sq = lambda a: a.reshape(bsz, seq, a.shape[-1])
        attn = _attn_prompt_call(sq(q), sq(k), sq(v), bias_p)
        hp, sattn_t = _post_prompt_call(
            l, hp, attn.reshape(bsz * seq, D_A), glu, c, mixw, post_w,
            (sq_t, sk_t, sv_t, bias_s, s_count, rb0, ck, cv), tm=tm_prompt, seq=seq)
        to_cache = lambda a: a.reshape(bsz, N_HEADS, HEAD_DIM, keep).transpose(0, 3, 1, 2)
        outs["pk"].append(to_cache(k_t))
        outs["pv"].append(to_cache(v_t))
        outs["pc"].append(sq(glu)[:, seq - (CONV_WIDTH - 1):])
        outs["pp"].append(sq(c)[:, seq - POOL_PREFIX:])

        hs = _post_sample_call(l, hs, sattn_t, sglu, sc, cct, cpt, mixw, post_w)
        outs["sk"].append(sk.reshape(nseq, 1, N_HEADS, HEAD_DIM))
        outs["sv"].append(sv.reshape(nseq, 1, N_HEADS, HEAD_DIM))
        outs["sc"].append(jnp.concatenate([cache_conv[l][:, 1:], sglu[:, None, :]], axis=1))
        outs["sp"].append(jnp.concatenate([cache_pool[l][:, 1:], sc[:, None, :]], axis=1))

    st = lambda n: jnp.stack(outs[n])
    return (hp.reshape(bsz, seq, D_MODEL), hs.reshape(nseq, 1, D_MODEL),
            st("pk"), st("pv"), st("pc"), st("pp"), st("sk"), st("sv"), st("sc"), st("sp"))
```

```python
import functools
import math

import numpy as np
import jax
import jax.numpy as jnp
from jax import lax
from jax.experimental import pallas as pl
from jax.experimental.pallas import tpu as pltpu

F32 = jnp.float32
BF16 = jnp.bfloat16

D_MODEL = 1024
HEAD_DIM = 64
N_HEADS = 8
D_A = N_HEADS * HEAD_DIM
D_B = 256
D_C = 256
D_IN = 3 * D_A + 2 * D_B + D_C
D_FF = 2816
DILATIONS = (1, 4, 16)
SPAN = 128
BLK = 128
CONV_WIDTH = 31
POOL_WINDOWS = (2, 4, 8, 16)
POOL_GROUP = 64
POOL_PREFIX = 15
NUM_BUCKETS = 32
MAX_EXACT = 16
REL_MAX_DIST = 2048
EPS = 1e-6
NEG_INF = -1e30

HEADS_PER_VREG = 128 // HEAD_DIM
N_HEAD_PAIRS = N_HEADS // HEADS_PER_VREG
HALO = 32
FF_CHUNK = 256
ATTN_GROUP = 4
SA_UNITS_PER_SEQ = 4
SA_SLOTS = 2
LOG2E = math.log2(math.e)
VMEM_LIMIT = 60 * 1024 * 1024


def _bucket_np(dist):
    dist = np.asarray(dist, np.int64)
    ratio = np.log(np.maximum(dist, 1).astype(np.float32) / np.float32(MAX_EXACT))
    large = MAX_EXACT + (ratio / np.float32(math.log(REL_MAX_DIST / MAX_EXACT))
                         * np.float32(NUM_BUCKETS - MAX_EXACT)).astype(np.int32)
    return np.where(dist < MAX_EXACT, dist, np.minimum(large, NUM_BUCKETS - 1)).astype(np.int32)


def _prompt_bucket_table():
    qi = np.arange(BLK)[:, None]
    kk = np.arange(2 * BLK)[None, :]
    dist = BLK + qi - kk
    ok = (dist >= 0) & (dist <= SPAN)
    tabs = []
    for dil in DILATIONS:
        b = _bucket_np(dil * np.clip(dist, 0, SPAN))
        tabs.append(np.where(ok, b, -1))
    return np.stack(tabs).astype(np.int32)


def _sample_tables(wbuf):
    dist = wbuf - np.arange(wbuf)
    count = np.zeros(wbuf, np.int32)
    for dil in DILATIONS:
        count += ((dist % dil == 0) & (dist // dil <= SPAN)).astype(np.int32)
    bucket = np.where(count > 0, _bucket_np(dist), -1).astype(np.int32)
    return bucket[None, :], count.astype(np.float32)[None, :]


def _dot(a, b):
    return jnp.dot(a, b, preferred_element_type=F32)


def _rms(x, g):
    ms = jnp.mean(x * x, axis=-1, keepdims=True)
    return x * lax.rsqrt(ms + EPS) * g


def _silu(x):
    return x * jax.nn.sigmoid(x)


def _swiglu_residual(h, g_ref, wgu_ref, wdn_ref, side_work=()):
    xn = _rms(h, g_ref[...]).astype(BF16)
    n_chunks = D_FF // FF_CHUNK
    assert len(side_work) <= n_chunks
    at_chunk = {(n * n_chunks) // len(side_work): w for n, w in enumerate(side_work)}
    acc = None
    for c in range(n_chunks):
        wait, compute, start = at_chunk.get(c, (None, None, None))
        if wait is not None:
            wait()
        lo = c * FF_CHUNK
        g = _dot(xn, wgu_ref[:, lo:lo + FF_CHUNK])
        u = _dot(xn, wgu_ref[:, D_FF + lo:D_FF + lo + FF_CHUNK])
        a = (_silu(g) * u).astype(BF16)
        d = _dot(a, wdn_ref[lo:lo + FF_CHUNK, :])
        acc = d if acc is None else acc + d
        if compute is not None:
            compute()
            start()
    return h + 0.5 * acc


def _head_rms(x, bd_ref, g):
    ms = _dot((x * x).astype(BF16), bd_ref[...])
    return x * lax.rsqrt(ms + EPS) * g


def _layernorm(x, g, b):
    mu = jnp.mean(x, axis=-1, keepdims=True)
    xc = x - mu
    var = jnp.mean(xc * xc, axis=-1, keepdims=True)
    return xc * lax.rsqrt(var + EPS) * g + b


def _pre_kernel(h_ref, gf_ref, wgu_ref, wdn_ref, gm_ref, win_ref, gq_ref, gk_ref, bd_ref, *rest,
                n_t, sa=None):
    side_work = ()
    if sa is not None:
        sa_in, rest, sa_scratch = rest[:N_SA_INPUTS], rest[N_SA_INPUTS:-3], rest[-3:]
        rest, sa_ref = rest[:-1], rest[-1]
        layer, n_steps, seq_lo, n_host = sa
        side_work = _sample_attn_side_work(pl.program_id(0), n_steps, layer, seq_lo, n_host,
                                           sa_in, sa_ref, sa_scratch)
    h1_ref, q_ref, k_ref, v_ref, glu_ref, c_ref = rest[:6]
    t_refs = rest[6:]
    assert len(t_refs) == n_t
    h1 = _swiglu_residual(h_ref[...], gf_ref, wgu_ref, wdn_ref, side_work)
    h1_ref[...] = h1
    xn = _rms(h1, gm_ref[...]).astype(BF16)
    q = _head_rms(_dot(xn, win_ref[:, 0:D_A]), bd_ref, gq_ref[...]) * (HEAD_DIM ** -0.5)
    k = _head_rms(_dot(xn, win_ref[:, D_A:2 * D_A]), bd_ref, gk_ref[...])
    v = _dot(xn, win_ref[:, 2 * D_A:3 * D_A])
    q_ref[...] = q
    k_ref[...] = k
    v_ref[...] = v
    for t_ref, val in zip(t_refs, (q, k, v)[3 - len(t_refs):]):
        t_ref[...] = val.T.reshape(t_ref.shape)
    o = 3 * D_A
    b_val = _dot(xn, win_ref[:, o:o + D_B])
    b_gate = _dot(xn, win_ref[:, o + D_B:o + 2 * D_B])
    glu_ref[...] = b_val * jax.nn.sigmoid(b_gate)
    c_ref[...] = _dot(xn, win_ref[:, o + 2 * D_B:o + 2 * D_B + D_C])


def _const_spec(shape):
    nd = len(shape)
    return pl.BlockSpec(shape, lambda i, _nd=nd: (0,) * _nd, pipeline_mode=pl.Buffered(1))


def _layer_spec(shape, layer):
    nd = len(shape)
    return pl.BlockSpec((None,) + tuple(shape), lambda i, _nd=nd: (layer,) + (0,) * _nd,
                        pipeline_mode=pl.Buffered(1))


def _pre_call(layer, h, gf, wgu, wdn, gm, win, gq, gk, bd, tm, seq=None, keep=None, sample=None,
              sa_seqs=None):
    t = h.shape[0]
    n_steps = t // tm
    row = lambda w: pl.BlockSpec((tm, w), lambda i: (i, 0))
    in_specs = [row(D_MODEL), _const_spec((1, D_MODEL)), _layer_spec((D_MODEL, 2 * D_FF), layer),
                _layer_spec((D_FF, D_MODEL), layer), _const_spec((1, D_MODEL)),
                _layer_spec((D_MODEL, D_IN), layer), _const_spec((1, D_A)), _const_spec((1, D_A)),
                _const_spec((D_A, D_A))]
    out_specs = [row(D_MODEL), row(D_A), row(D_A), row(D_A), row(D_B), row(D_C)]
    out_shape = [jax.ShapeDtypeStruct((t, w), F32) for w in (D_MODEL, D_A, D_A, D_A, D_B, D_C)]
    if seq is not None:
        per_seq, skip = seq // tm, (seq - keep) // tm
        t_spec = pl.BlockSpec((1, D_A, tm),
                              lambda i: (i // per_seq, 0, jnp.maximum(i % per_seq - skip, 0)))
        out_specs += [t_spec] * 2
        out_shape += [jax.ShapeDtypeStruct((t // seq, D_A, keep), F32)] * 2
    else:
        out_specs += [pl.BlockSpec((D_A, tm), lambda i: (0, i))] * 3
        out_shape += [jax.ShapeDtypeStruct((D_A, t), F32)] * 3
    n_t = len(out_specs) - 6
    scratch, sa, operands = [], None, ()
    if sample is not None:
        sa_in, sa_out, sa_shape, scratch = _sample_attn_specs(sample)
        in_specs += sa_in
        out_specs.append(sa_out)
        out_shape.append(sa_shape)
        sa, operands = (layer, n_steps) + tuple(sa_seqs), tuple(sample)
    return pl.pallas_call(
        functools.partial(_pre_kernel, n_t=n_t, sa=sa),
        grid=(n_steps,),
        in_specs=in_specs,
        out_specs=out_specs,
        out_shape=out_shape,
        scratch_shapes=scratch,
        compiler_params=pltpu.CompilerParams(dimension_semantics=("arbitrary",),
                                             vmem_limit_bytes=VMEM_LIMIT),
        name="pre_ffn_inproj",
    )(h, gf, wgu, wdn, gm, win, gq, gk, bd, *operands)


def _pool_lane_consts(shape):
    grp = lax.broadcasted_iota(jnp.int32, shape, len(shape) - 1) // POOL_GROUP
    win = jnp.full(shape, POOL_WINDOWS[0], jnp.int32)
    for g in range(1, len(POOL_WINDOWS)):
        win = jnp.where(grp == g, POOL_WINDOWS[g], win)
    return win


def _conv_tail(y, cb_ref, lng_ref, lnb_ref, pw_ref):
    y = _layernorm(y + cb_ref[...], lng_ref[...], lnb_ref[...])
    return _dot(_silu(y).astype(BF16), pw_ref[...])


def _pool_tail(tot, x, cnt, plw_ref, psc_ref):
    d = tot / cnt - x
    return _dot(d.astype(BF16), plw_ref[...]) * psc_ref[...]


def _mix_ffn(h, attn, conv, pool, wout_ref, gf_ref, wgu_ref, wdn_ref, side_work=()):
    mix = _dot(attn.astype(BF16), wout_ref[0:D_A, :])
    mix = mix + _dot(conv.astype(BF16), wout_ref[D_A:D_A + D_B, :])
    mix = mix + _dot(pool.astype(BF16), wout_ref[D_A + D_B:D_A + D_B + D_C, :])
    return _swiglu_residual(h + mix, gf_ref, wgu_ref, wdn_ref, side_work)


N_SA_INPUTS = 9


def _sample_attn_side_work(i, n_steps, layer, seq_lo, n_host, in_refs, o_ref, scratch):
    qt_ref, kt_ref, vt_ref, sbias_ref, cnt_ref, rb0_ref, init_ref, ck_hbm, cv_hbm = in_refs
    kbuf, vbuf, sems = scratch
    n_seq = qt_ref.shape[1]
    units = SA_UNITS_PER_SEQ * n_host // n_steps
    assert units * n_steps == SA_UNITS_PER_SEQ * n_host and units % SA_SLOTS == 0
    heads = N_HEADS // SA_UNITS_PER_SEQ

    def seq_of(step, j):
        return seq_lo + step * (units // SA_UNITS_PER_SEQ) + j // SA_UNITS_PER_SEQ

    def copies(step, j):
        g, slot = j % SA_UNITS_PER_SEQ, j % SA_SLOTS
        return [pltpu.make_async_copy(hbm.at[layer, seq_of(step, j), pl.ds(g * heads, heads)],
                                      buf.at[slot], sems.at[n, slot])
                for n, (hbm, buf) in enumerate(((ck_hbm, kbuf), (cv_hbm, vbuf)))]

    def start(step, j):
        for cp in copies(step, j):
            cp.start()

    @pl.when(i == 0)
    def _():
        o_ref[...] = init_ref[...]
        for j in range(SA_SLOTS):
            start(i, j)

    lane = lax.broadcasted_iota(jnp.int32, (D_A, n_seq), 1)
    head = lambda a, h: a[h * HEAD_DIM:(h + 1) * HEAD_DIM]
    cols = {}

    def wait(j):
        for cp in copies(i, j):
            cp.wait()

    def refill(j):
        nxt = j + SA_SLOTS
        if nxt < units:
            start(i, nxt)
        else:
            @pl.when(i + 1 < n_steps)
            def _():
                start(i + 1, nxt - units)

    def unit(j):
        s = seq_of(i, j)
        g, slot = j % SA_UNITS_PER_SEQ, j % SA_SLOTS
        sel = lane == s
        if j // SA_UNITS_PER_SEQ not in cols:
            cols[j // SA_UNITS_PER_SEQ] = [
                jnp.sum(jnp.where(sel, r[...], 0.0), axis=1, keepdims=True)
                for r in (qt_ref, kt_ref, vt_ref)]
        qc, kc, vc = cols[j // SA_UNITS_PER_SEQ]
        h0 = g * heads
        rows, new = [], []
        for hh in range(heads):
            rows.append(jnp.sum(kbuf[slot, hh] * head(qc, h0 + hh), axis=0, keepdims=True))
            new.append(jnp.sum(head(qc, h0 + hh) * head(kc, h0 + hh), axis=0, keepdims=True))
        lg = jnp.concatenate(rows, axis=0) + sbias_ref[h0:h0 + heads, :]
        lg0 = jnp.concatenate(new, axis=0) + rb0_ref[h0:h0 + heads, :]
        m = jnp.maximum(jnp.max(lg, axis=1, keepdims=True), lg0)
        p = jnp.exp(lg - m) * cnt_ref[...]
        p0 = len(DILATIONS) * jnp.exp(lg0 - m)
        den = jnp.sum(p, axis=1, keepdims=True) + p0
        outs = []
        for hh in range(heads):
            num = jnp.sum(vbuf[slot, hh] * p[hh:hh + 1, :], axis=1, keepdims=True)
            num = num + p0[hh:hh + 1] * head(vc, h0 + hh)
            outs.append(num / den[hh:hh + 1])
        r0, r1 = h0 * HEAD_DIM, (h0 + heads) * HEAD_DIM
        mine = lax.broadcasted_iota(jnp.int32, (r1 - r0, n_seq), 1) == s
        o_ref[r0:r1, :] = jnp.where(mine, jnp.concatenate(outs, axis=0), o_ref[r0:r1, :])

    return [tuple(functools.partial(f, j) for f in (wait, unit, refill)) for j in range(units)]


def _post_prompt_kernel(h_ref, attn_ref, g_ref, gh_ref, c_ref, ch_ref,
                        cw_ref, cb_ref, lng_ref, lnb_ref, pw_ref, plw_ref, psc_ref,
                        wout_ref, gf_ref, wgu_ref, wdn_ref, *rest,
                        tm, per_seq, layer, n_steps, sa_seqs):
    sa_in, (out_ref, sa_ref, gx, cx), sa_scratch = rest[:N_SA_INPUTS], rest[N_SA_INPUTS:-3], rest[-3:]
    i = pl.program_id(0)
    side_work = _sample_attn_side_work(i, n_steps, layer, *sa_seqs, sa_in, sa_ref, sa_scratch)
    tile = i % per_seq
    keep = (tile > 0).astype(F32)
    gx[0:HALO, :] = gh_ref[...] * keep
    gx[HALO:HALO + tm, :] = g_ref[...]
    cx[0:HALO, :] = ch_ref[...] * keep
    cx[HALO:HALO + tm, :] = c_ref[...]

    ext = tm + 8
    y = None
    for b in range(8):
        part = None
        for a in range(-(-CONV_WIDTH // 8)):
            lag = 8 * a + b
            if lag >= CONV_WIDTH:
                continue
            lo = HALO - 8 - 8 * a
            term = cw_ref[CONV_WIDTH - 1 - lag:CONV_WIDTH - lag, :] * gx[lo:lo + ext, :]
            part = term if part is None else part + term
        if b:
            part = pltpu.roll(part, b, 0)
        y = part if y is None else y + part
    conv = _conv_tail(y[8:], cb_ref, lng_ref, lnb_ref, pw_ref)

    x = c_ref[...]
    win = _pool_lane_consts((tm, D_C))
    run = cx[...]
    tot = jnp.zeros_like(x)
    prev_w = 1
    for w in POOL_WINDOWS:
        assert w == 2 * prev_w
        run = run + pltpu.roll(run, prev_w, 0)
        tot = jnp.where(win == w, run[HALO:], tot)
        prev_w = w
    pos = tile * tm + lax.broadcasted_iota(jnp.int32, (tm, D_C), 0)
    cnt = jnp.minimum(win, pos + 1).astype(F32)
    pool = _pool_tail(tot, x, cnt, plw_ref, psc_ref)

    out_ref[...] = _mix_ffn(h_ref[...], attn_ref[...], conv, pool, wout_ref, gf_ref, wgu_ref, wdn_ref,
                            side_work)


def _post_sample_kernel(h_ref, attn_ref, g_ref, c_ref, cc_ref, cp_ref,
                        cw_ref, cb_ref, lng_ref, lnb_ref, pw_ref, plw_ref, psc_ref,
                        wout_ref, gf_ref, wgu_ref, wdn_ref, out_ref):
    n_hist = CONV_WIDTH - 1
    y = g_ref[...] * cw_ref[n_hist:n_hist + 1, :]
    for j in range(n_hist):
        y = y + cc_ref[0, j] * cw_ref[j:j + 1, :]
    conv = _conv_tail(y, cb_ref, lng_ref, lnb_ref, pw_ref)

    x = c_ref[...]
    win = _pool_lane_consts(x.shape)
    run = x
    tot = jnp.zeros_like(x)
    prev_w = 1
    for w in POOL_WINDOWS:
        for sft in range(prev_w, w):
            run = run + cp_ref[0, POOL_PREFIX - sft]
        tot = jnp.where(win == w, run, tot)
        prev_w = w
    pool = _pool_tail(tot, x, win.astype(F32), plw_ref, psc_ref)

    out_ref[...] = _mix_ffn(h_ref[...], attn_ref[...].T, conv, pool, wout_ref, gf_ref, wgu_ref, wdn_ref)


def _mixer_weight_specs(layer):
    return [_const_spec((CONV_WIDTH, D_B)), _const_spec((1, D_B)), _const_spec((1, D_B)),
            _const_spec((1, D_B)), _const_spec((D_B, D_B)), _const_spec((D_C, D_C)),
            _const_spec((1, D_C)),
            _layer_spec((D_MODEL, D_MODEL), layer), _const_spec((1, D_MODEL)),
            _layer_spec((D_MODEL, 2 * D_FF), layer), _layer_spec((D_FF, D_MODEL), layer)]


def _sample_attn_specs(sample):
    assert len(sample) == N_SA_INPUTS
    n_seq, wbuf = sample[0].shape[1], sample[-1].shape[-1]
    heads = N_HEADS // SA_UNITS_PER_SEQ
    cst = lambda shape: pl.BlockSpec(shape, lambda i: (0,) * len(shape))
    hbm = pl.BlockSpec(memory_space=pl.ANY)
    in_specs = [cst((D_A, n_seq))] * 3 + [cst((N_HEADS, wbuf)), cst((1, wbuf)), cst((N_HEADS, 1)),
                                          cst((D_A, n_seq)), hbm, hbm]
    scratch = [pltpu.VMEM((SA_SLOTS, heads, HEAD_DIM, wbuf), F32),
               pltpu.VMEM((SA_SLOTS, heads, HEAD_DIM, wbuf), F32),
               pltpu.SemaphoreType.DMA((2, SA_SLOTS))]
    return in_specs, cst((D_A, n_seq)), jax.ShapeDtypeStruct((D_A, n_seq), F32), scratch


def _post_prompt_call(layer, h, attn, glu, c, mixw, postw, sample, sa_seqs, tm, seq):
    t = h.shape[0]
    per = tm // HALO
    n_steps = t // tm
    row = lambda w: pl.BlockSpec((tm, w), lambda i: (i, 0))
    halo = lambda w: pl.BlockSpec((HALO, w), lambda i: (jnp.maximum(i * per - 1, 0), 0))
    sa_in, sa_out, sa_shape, sa_scratch = _sample_attn_specs(sample)
    return pl.pallas_call(
        functools.partial(_post_prompt_kernel, tm=tm, per_seq=seq // tm, layer=layer,
                          n_steps=n_steps, sa_seqs=sa_seqs),
        grid=(n_steps,),
        in_specs=[row(D_MODEL), row(D_A), row(D_B), halo(D_B), row(D_C), halo(D_C)]
        + _mixer_weight_specs(layer) + sa_in,
        out_specs=[row(D_MODEL), sa_out],
        out_shape=[jax.ShapeDtypeStruct((t, D_MODEL), F32), sa_shape],
        scratch_shapes=[pltpu.VMEM((HALO + tm, D_B), F32), pltpu.VMEM((HALO + tm, D_C), F32)]
        + sa_scratch,
        compiler_params=pltpu.CompilerParams(dimension_semantics=("arbitrary",),
                                             vmem_limit_bytes=VMEM_LIMIT),
        name="post_prompt",
    )(h, attn, glu, glu, c, c, *mixw, *postw, *sample)


def _post_sample_call(layer, h, attn, glu, c, cct, cpt, mixw, postw):
    t = h.shape[0]
    full = lambda w: pl.BlockSpec((t, w), lambda i: (0, 0))
    return pl.pallas_call(
        _post_sample_kernel,
        grid=(1,),
        in_specs=[full(D_MODEL), pl.BlockSpec((D_A, t), lambda i: (0, 0)), full(D_B), full(D_C),
                  pl.BlockSpec((1, CONV_WIDTH - 1, t, D_B), lambda i: (layer, 0, 0, 0)),
                  pl.BlockSpec((1, POOL_PREFIX, t, D_C), lambda i: (layer, 0, 0, 0))]
        + _mixer_weight_specs(layer),
        out_specs=full(D_MODEL),
        out_shape=jax.ShapeDtypeStruct((t, D_MODEL), F32),
        compiler_params=pltpu.CompilerParams(dimension_semantics=("arbitrary",),
                                             vmem_limit_bytes=VMEM_LIMIT),
        name="post_sample",
    )(h, attn, glu, c, cct, cpt, *mixw, *postw)


def _bias_prompt_kernel(rb_ref, tbl_ref, out_ref):
    h = pl.program_id(1)
    tbl = tbl_ref[0]

    def body(b, acc):
        return jnp.where(tbl == b, rb_ref[b, h], acc)

    acc = lax.fori_loop(0, NUM_BUCKETS, body, jnp.zeros(tbl.shape, F32))
    out_ref[0, 0] = jnp.where(tbl < 0, NEG_INF, acc * LOG2E)


def _bias_prompt_call(rel_bias):
    tbl = jnp.asarray(_prompt_bucket_table())
    nbr = len(DILATIONS)
    return pl.pallas_call(
        _bias_prompt_kernel,
        grid=(nbr, N_HEADS),
        in_specs=[pl.BlockSpec(memory_space=pltpu.SMEM),
                  pl.BlockSpec((1, BLK, 2 * BLK), lambda br, h: (br, 0, 0))],
        out_specs=pl.BlockSpec((1, 1, BLK, 2 * BLK), lambda br, h: (br, h, 0, 0)),
        out_shape=jax.ShapeDtypeStruct((nbr, N_HEADS, BLK, 2 * BLK), F32),
        name="bias_prompt",
    )(rel_bias, tbl)


def _bias_sample_kernel(rb_ref, tbl_ref, out_ref):
    tbl = tbl_ref[...]
    for h in range(N_HEADS):
        def body(b, acc, h=h):
            return jnp.where(tbl == b, rb_ref[b, h], acc)

        acc = lax.fori_loop(0, NUM_BUCKETS, body, jnp.zeros(tbl.shape, F32))
        out_ref[h:h + 1, :] = jnp.where(tbl < 0, NEG_INF, acc)


def _bias_sample_call(rel_bias, bucket):
    return pl.pallas_call(
        _bias_sample_kernel,
        in_specs=[pl.BlockSpec(memory_space=pltpu.SMEM),
                  pl.BlockSpec(bucket.shape, lambda: (0, 0))],
        out_shape=jax.ShapeDtypeStruct((N_HEADS, bucket.shape[1]), F32),
        name="bias_sample",
    )(rel_bias, bucket)


def _rows(start, size, dil):
    return pl.ds(start, size) if dil == 1 else pl.ds(start, size, stride=dil)


def _attn_prompt_kernel(q_ref, k_ref, v_ref, bias_ref, o_ref, m_acc, s_acc, n_acc, *, seq):
    lane_lo = lax.broadcasted_iota(jnp.int32, (BLK, 128), 1) < HEAD_DIM

    def group(br, dil, starts, n_keys, last):
        bias = bias_ref[br, 0, :, 2 * BLK - n_keys:]
        logits, values = [], []
        for q_start, k_start in starts:
            q = q_ref[0, _rows(q_start, BLK, dil), :] * LOG2E
            qs = jnp.concatenate([jnp.where(lane_lo, q, 0.0), jnp.where(lane_lo, 0.0, q)], axis=0)
            kb = k_ref[0, _rows(k_start, n_keys, dil), :].astype(BF16)
            values.append(v_ref[0, _rows(k_start, n_keys, dil), :].astype(BF16))
            logits.append(lax.dot_general(qs.astype(BF16), kb, (((1,), (1,)), ((), ())),
                                          preferred_element_type=F32) + bias)
        stats = []
        for lg, vb in zip(logits, values):
            m = jnp.max(lg, axis=-1, keepdims=True)
            p = jnp.exp2(lg - m)
            s = jnp.sum(p, axis=-1, keepdims=True)
            pv = _dot(p.astype(BF16), vb)
            stats.append((jnp.where(lane_lo, m[:BLK], m[BLK:]),
                          jnp.where(lane_lo, s[:BLK], s[BLK:]),
                          jnp.where(lane_lo, pv[:BLK], pv[BLK:])))
        rows = [_rows(q_start, BLK, dil) for q_start, _ in starts]
        if br == 0:
            for r, (mm, ss, num) in zip(rows, stats):
                m_acc[r, :] = mm
                s_acc[r, :] = ss
                n_acc[r, :] = num
            return
        old = [(m_acc[r, :], s_acc[r, :], n_acc[r, :]) for r in rows]
        for r, (mm, ss, num), (m_old, s_old, n_old) in zip(rows, stats, old):
            m_new = jnp.maximum(m_old, mm)
            a = jnp.exp2(m_old - m_new)
            b = jnp.exp2(mm - m_new)
            s_new = a * s_old + b * ss
            n_new = a * n_old + b * num
            if last:
                o_ref[0, r, :] = n_new / s_new
            else:
                m_acc[r, :] = m_new
                s_acc[r, :] = s_new
                n_acc[r, :] = n_new

    for br, dil in enumerate(DILATIONS):
        last = br == len(DILATIONS) - 1
        nb = seq // dil // BLK

        def starts_of(u, first, dil=dil):
            if first:
                return u, u
            lb = u // dil + 1
            r = u % dil
            return r + dil * BLK * lb, r + dil * BLK * (lb - 1)

        for first, n_units in ((True, dil), (False, dil * (nb - 1))):
            n_keys = BLK if first else 2 * BLK
            rem = n_units % ATTN_GROUP
            if rem:
                group(br, dil, [starts_of(u, first) for u in range(rem)], n_keys, last)

            def body(g, carry, br=br, dil=dil, last=last, first=first, rem=rem, n_keys=n_keys,
                     starts_of=starts_of):
                u0 = rem + g * ATTN_GROUP
                group(br, dil, [starts_of(u0 + j, first) for j in range(ATTN_GROUP)], n_keys, last)
                return carry

            lax.fori_loop(0, n_units // ATTN_GROUP, body, 0)


def _attn_prompt_call(q, k, v, bias2):
    bsz, seq, _ = q.shape
    blk = pl.BlockSpec((1, seq, 128), lambda b, hp: (b, 0, hp))
    return pl.pallas_call(
        functools.partial(_attn_prompt_kernel, seq=seq),
        grid=(bsz, N_HEAD_PAIRS),
        in_specs=[blk, blk, blk,
                  pl.BlockSpec((len(DILATIONS), 1, HEADS_PER_VREG * BLK, 2 * BLK),
                               lambda b, hp: (0, hp, 0, 0))],
        out_specs=blk,
        out_shape=jax.ShapeDtypeStruct((bsz, seq, D_A), F32),
        scratch_shapes=[pltpu.VMEM((seq, 128), F32)] * 3,
        compiler_params=pltpu.CompilerParams(dimension_semantics=("arbitrary", "arbitrary"),
                                             vmem_limit_bytes=VMEM_LIMIT),
        name="attn_prompt",
    )(q, k, v, bias2)


def _block_diag(blocks):
    n, r, c = blocks.shape
    eye = jnp.eye(n, dtype=blocks.dtype)
    return (eye[:, None, :, None] * blocks[:, :, None, :]).reshape(n * r, n * c)


def kernel(x_prompt, x_sample, cache_attn_k, cache_attn_v, cache_conv, cache_pool, rel_bias, g_ffn1, w_ffn1_gu, w_ffn1_down, g_mix, w_in, g_q, g_k, conv_w, conv_b, conv_ln_g, conv_ln_b, conv_pw, pool_w, pool_scale, w_out, g_ffn2, w_ffn2_gu, w_ffn2_down):
    bsz, seq, _ = x_prompt.shape
    nseq = x_sample.shape[0]
    depth = g_ffn1.shape[0]
    wbuf = cache_attn_k.shape[2]
    assert x_sample.shape[1] == 1 and wbuf == DILATIONS[-1] * SPAN and nseq == 128
    tm_prompt, tm_sample = 512, nseq

    head_of_lane = np.arange(D_A) // HEAD_DIM
    bd = jnp.asarray((head_of_lane[:, None] == head_of_lane[None, :]) / HEAD_DIM, BF16)
    bias_p = _bias_prompt_call(rel_bias).reshape(
        len(DILATIONS), N_HEAD_PAIRS, HEADS_PER_VREG * BLK, 2 * BLK)
    s_bucket, s_count = _sample_tables(wbuf)
    bias_s = _bias_sample_call(rel_bias, jnp.asarray(s_bucket))
    s_count = jnp.asarray(s_count)
    rb0 = rel_bias[0].reshape(N_HEADS, 1)
    ck = jnp.transpose(cache_attn_k, (0, 1, 3, 4, 2))
    cv = jnp.transpose(cache_attn_v, (0, 1, 3, 4, 2))
    cct = jnp.transpose(cache_conv, (0, 2, 1, 3))
    cpt = jnp.transpose(cache_pool, (0, 2, 1, 3))

    hp = x_prompt.reshape(bsz * seq, D_MODEL)
    hs = x_sample.reshape(nseq, D_MODEL)
    outs = {n: [] for n in ("pk", "pv", "pc", "pp", "sk", "sv", "sc", "sp")}
    wgu1, wdn1, win = (w.astype(BF16) for w in (w_ffn1_gu, w_ffn1_down, w_in))
    wgu2, wdn2, wout = (w.astype(BF16) for w in (w_ffn2_gu, w_ffn2_down, w_out))
    half = nseq // 2
    for l in range(depth):
        r1 = lambda a: a[l].reshape(1, -1)
        gq = jnp.tile(g_q[l], N_HEADS).reshape(1, D_A)
        gk = jnp.tile(g_k[l], N_HEADS).reshape(1, D_A)
        pw = conv_pw[l].astype(BF16)
        plw = _block_diag(pool_w[l]).astype(BF16)
        mixw = (conv_w[l], r1(conv_b), r1(conv_ln_g), r1(conv_ln_b), pw, plw, r1(pool_scale))
        pre_w = (r1(g_ffn1), wgu1, wdn1, r1(g_mix), win, gq, gk, bd)
        post_w = (wout, r1(g_ffn2), wgu2, wdn2)

        keep = min(wbuf, seq)
        hs, _, sk, sv, sglu, sc, sq_t, sk_t, sv_t = _pre_call(l, hs, *pre_w, tm=tm_sample)
        sample = lambda init: (sq_t, sk_t, sv_t, bias_s, s_count, rb0, init, ck, cv)
        hp, q, k, v, glu, c, k_t, v_t, sattn_t = _pre_call(
            l, hp, *pre_w, tm=tm_prompt, seq=seq, keep=keep,
            sample=sample(jnp.zeros((D_A, nseq), F32)), sa_seqs=(0, half))

        sq = lambda a: a.reshape(bsz, seq, a.shape[-1])
        attn = _attn_prompt_call(sq(q), sq(k), sq(v), bias_p)
        hp, sattn_t = _post_prompt_call(
            l, hp, attn.reshape(bsz * seq, D_A), glu, c, mixw, post_w,
            sample(sattn_t), (half, nseq - half), tm=tm_prompt, seq=seq)
        to_cache = lambda a: a.reshape(bsz, N_HEADS, HEAD_DIM, keep).transpose(0, 3, 1, 2)
        outs["pk"].append(to_cache(k_t))
        outs["pv"].append(to_cache(v_t))
        outs["pc"].append(sq(glu)[:, seq - (CONV_WIDTH - 1):])
        outs["pp"].append(sq(c)[:, seq - POOL_PREFIX:])

        hs = _post_sample_call(l, hs, sattn_t, sglu, sc, cct, cpt, mixw, post_w)
        outs["sk"].append(sk.reshape(nseq, 1, N_HEADS, HEAD_DIM))
        outs["sv"].append(sv.reshape(nseq, 1, N_HEADS, HEAD_DIM))
        outs["sc"].append(jnp.concatenate([cache_conv[l][:, 1:], sglu[:, None, :]], axis=1))
        outs["sp"].append(jnp.concatenate([cache_pool[l][:, 1:], sc[:, None, :]], axis=1))

    st = lambda n: jnp.stack(outs[n])
    return (hp.reshape(bsz, seq, D_MODEL), hs.reshape(nseq, 1, D_MODEL),
            st("pk"), st("pv"), st("pc"), st("pp"), st("sk"), st("sv"), st("sc"), st("sp"))
```

```python
import functools
import math

import numpy as np
import jax
import jax.numpy as jnp
from jax import lax
from jax.experimental import pallas as pl
from jax.experimental.pallas import tpu as pltpu

F32 = jnp.float32
BF16 = jnp.bfloat16

D_MODEL = 1024
HEAD_DIM = 64
N_HEADS = 8
D_A = N_HEADS * HEAD_DIM
D_B = 256
D_C = 256
D_IN = 3 * D_A + 2 * D_B + D_C
D_FF = 2816
DILATIONS = (1, 4, 16)
SPAN = 128
BLK = 128
CONV_WIDTH = 31
POOL_WINDOWS = (2, 4, 8, 16)
POOL_GROUP = 64
POOL_PREFIX = 15
NUM_BUCKETS = 32
MAX_EXACT = 16
REL_MAX_DIST = 2048
EPS = 1e-6
NEG_INF = -1e30

HEADS_PER_VREG = 128 // HEAD_DIM
N_HEAD_PAIRS = N_HEADS // HEADS_PER_VREG
HALO = 32
FF_CHUNK = 256
ATTN_GROUP = 8
SA_UNITS_PER_SEQ = 1
SA_SLOTS = 2
LOG2E = math.log2(math.e)
VMEM_LIMIT = 60 * 1024 * 1024


def _bucket_np(dist):
    dist = np.asarray(dist, np.int64)
    ratio = np.log(np.maximum(dist, 1).astype(np.float32) / np.float32(MAX_EXACT))
    large = MAX_EXACT + (ratio / np.float32(math.log(REL_MAX_DIST / MAX_EXACT))
                         * np.float32(NUM_BUCKETS - MAX_EXACT)).astype(np.int32)
    return np.where(dist < MAX_EXACT, dist, np.minimum(large, NUM_BUCKETS - 1)).astype(np.int32)


def _prompt_bucket_table():
    qi = np.arange(BLK)[:, None]
    kk = np.arange(2 * BLK)[None, :]
    dist = BLK + qi - kk
    ok = (dist >= 0) & (dist <= SPAN)
    tabs = []
    for dil in DILATIONS:
        b = _bucket_np(dil * np.clip(dist, 0, SPAN))
        tabs.append(np.where(ok, b, -1))
    return np.stack(tabs).astype(np.int32)


def _sample_tables(wbuf):
    dist = wbuf - np.arange(wbuf)
    count = np.zeros(wbuf, np.int32)
    for dil in DILATIONS:
        count += ((dist % dil == 0) & (dist // dil <= SPAN)).astype(np.int32)
    bucket = np.where(count > 0, _bucket_np(dist), -1).astype(np.int32)
    return bucket[None, :], count.astype(np.float32)[None, :]


def _dot(a, b):
    return jnp.dot(a, b, preferred_element_type=F32)


def _rms(x, g):
    ms = jnp.mean(x * x, axis=-1, keepdims=True)
    return x * lax.rsqrt(ms + EPS) * g


def _silu(x):
    return x * jax.nn.sigmoid(x)


FFN_WEAVE_SLOTS = 3 * (D_FF // FF_CHUNK)


def _weave_plan(units):
    plan = {}
    for n, (wait, work) in enumerate(units):
        chunk = (n * (FFN_WEAVE_SLOTS // 3)) // len(units)
        assert 3 * chunk - 1 not in plan
        plan[3 * chunk - 1] = wait
        plan[3 * chunk + 2] = work
    return plan


def _swiglu_residual(h, g_ref, wgu_ref, wdn_ref, weave=None):
    weave = weave or {}
    xn = _rms(h, g_ref[...]).astype(BF16)
    if -1 in weave:
        weave[-1]()
    acc = None
    for c in range(D_FF // FF_CHUNK):
        lo = c * FF_CHUNK

        def after(n, c=c):
            if 3 * c + n in weave:
                weave[3 * c + n]()

        g = _dot(xn, wgu_ref[:, lo:lo + FF_CHUNK])
        after(0)
        u = _dot(xn, wgu_ref[:, D_FF + lo:D_FF + lo + FF_CHUNK])
        after(1)
        a = (_silu(g) * u).astype(BF16)
        d = _dot(a, wdn_ref[lo:lo + FF_CHUNK, :])
        acc = d if acc is None else acc + d
        after(2)
    return h + 0.5 * acc


def _head_rms(x, bd_ref, g):
    ms = _dot((x * x).astype(BF16), bd_ref[...])
    return x * lax.rsqrt(ms + EPS) * g


def _layernorm(x, g, b):
    mu = jnp.mean(x, axis=-1, keepdims=True)
    xc = x - mu
    var = jnp.mean(xc * xc, axis=-1, keepdims=True)
    return xc * lax.rsqrt(var + EPS) * g + b


def _pre_kernel(h_ref, gf_ref, wgu_ref, wdn_ref, gm_ref, win_ref, gq_ref, gk_ref, bd_ref, *rest,
                n_t, sa=None):
    weave = None
    if sa is not None:
        sa_in, rest, sa_scratch = rest[:N_SA_INPUTS], rest[N_SA_INPUTS:-3], rest[-3:]
        rest, sa_ref = rest[:-1], rest[-1]
        layer, n_steps, seq_lo, n_host = sa
        weave = _weave_plan(_sample_attn_units(pl.program_id(0), n_steps, layer, seq_lo, n_host,
                                               sa_in, sa_ref, sa_scratch))
    h1_ref, q_ref, k_ref, v_ref, glu_ref, c_ref = rest[:6]
    t_refs = rest[6:]
    assert len(t_refs) == n_t
    h1 = _swiglu_residual(h_ref[...], gf_ref, wgu_ref, wdn_ref, weave)
    h1_ref[...] = h1
    xn = _rms(h1, gm_ref[...]).astype(BF16)
    q = _head_rms(_dot(xn, win_ref[:, 0:D_A]), bd_ref, gq_ref[...]) * (HEAD_DIM ** -0.5)
    k = _head_rms(_dot(xn, win_ref[:, D_A:2 * D_A]), bd_ref, gk_ref[...])
    v = _dot(xn, win_ref[:, 2 * D_A:3 * D_A])
    q_ref[...] = q
    k_ref[...] = k
    v_ref[...] = v
    for t_ref, val in zip(t_refs, (q, k, v)[3 - len(t_refs):]):
        t_ref[...] = val.T.reshape(t_ref.shape)
    o = 3 * D_A
    b_val = _dot(xn, win_ref[:, o:o + D_B])
    b_gate = _dot(xn, win_ref[:, o + D_B:o + 2 * D_B])
    glu_ref[...] = b_val * jax.nn.sigmoid(b_gate)
    c_ref[...] = _dot(xn, win_ref[:, o + 2 * D_B:o + 2 * D_B + D_C])


def _const_spec(shape):
    nd = len(shape)
    return pl.BlockSpec(shape, lambda i, _nd=nd: (0,) * _nd, pipeline_mode=pl.Buffered(1))


def _layer_spec(shape, layer):
    nd = len(shape)
    return pl.BlockSpec((None,) + tuple(shape), lambda i, _nd=nd: (layer,) + (0,) * _nd,
                        pipeline_mode=pl.Buffered(1))


def _pre_call(layer, h, gf, wgu, wdn, gm, win, gq, gk, bd, tm, seq=None, keep=None, sample=None,
              sa_seqs=None):
    t = h.shape[0]
    n_steps = t // tm
    row = lambda w: pl.BlockSpec((tm, w), lambda i: (i, 0))
    in_specs = [row(D_MODEL), _const_spec((1, D_MODEL)), _layer_spec((D_MODEL, 2 * D_FF), layer),
                _layer_spec((D_FF, D_MODEL), layer), _const_spec((1, D_MODEL)),
                _layer_spec((D_MODEL, D_IN), layer), _const_spec((1, D_A)), _const_spec((1, D_A)),
                _const_spec((D_A, D_A))]
    out_specs = [row(D_MODEL), row(D_A), row(D_A), row(D_A), row(D_B), row(D_C)]
    out_shape = [jax.ShapeDtypeStruct((t, w), F32) for w in (D_MODEL, D_A, D_A, D_A, D_B, D_C)]
    if seq is not None:
        per_seq, skip = seq // tm, (seq - keep) // tm
        t_spec = pl.BlockSpec((1, D_A, tm),
                              lambda i: (i // per_seq, 0, jnp.maximum(i % per_seq - skip, 0)))
        out_specs += [t_spec] * 2
        out_shape += [jax.ShapeDtypeStruct((t // seq, D_A, keep), F32)] * 2
    else:
        out_specs += [pl.BlockSpec((D_A, tm), lambda i: (0, i))] * 3
        out_shape += [jax.ShapeDtypeStruct((D_A, t), F32)] * 3
    n_t = len(out_specs) - 6
    scratch, sa, operands = [], None, ()
    if sample is not None:
        sa_in, sa_out, sa_shape, scratch = _sample_attn_specs(sample)
        in_specs += sa_in
        out_specs.append(sa_out)
        out_shape.append(sa_shape)
        sa, operands = (layer, n_steps) + tuple(sa_seqs), tuple(sample)
    return pl.pallas_call(
        functools.partial(_pre_kernel, n_t=n_t, sa=sa),
        grid=(n_steps,),
        in_specs=in_specs,
        out_specs=out_specs,
        out_shape=out_shape,
        scratch_shapes=scratch,
        compiler_params=pltpu.CompilerParams(dimension_semantics=("arbitrary",),
                                             vmem_limit_bytes=VMEM_LIMIT),
        name="pre_ffn_inproj",
    )(h, gf, wgu, wdn, gm, win, gq, gk, bd, *operands)


def _pool_lane_consts(shape):
    grp = lax.broadcasted_iota(jnp.int32, shape, len(shape) - 1) // POOL_GROUP
    win = jnp.full(shape, POOL_WINDOWS[0], jnp.int32)
    for g in range(1, len(POOL_WINDOWS)):
        win = jnp.where(grp == g, POOL_WINDOWS[g], win)
    return win


def _conv_tail(y, cb_ref, lng_ref, lnb_ref, pw_ref):
    y = _layernorm(y + cb_ref[...], lng_ref[...], lnb_ref[...])
    return _dot(_silu(y).astype(BF16), pw_ref[...])


def _pool_tail(tot, x, cnt, plw_ref, psc_ref):
    d = tot / cnt - x
    return _dot(d.astype(BF16), plw_ref[...]) * psc_ref[...]


def _mix_ffn(h, attn, conv, pool, wout_ref, gf_ref, wgu_ref, wdn_ref, weave=None):
    mix = _dot(attn.astype(BF16), wout_ref[0:D_A, :])
    mix = mix + _dot(conv.astype(BF16), wout_ref[D_A:D_A + D_B, :])
    mix = mix + _dot(pool.astype(BF16), wout_ref[D_A + D_B:D_A + D_B + D_C, :])
    return _swiglu_residual(h + mix, gf_ref, wgu_ref, wdn_ref, weave)


N_SA_INPUTS = 9


def _sample_attn_units(i, n_steps, layer, seq_lo, n_host, in_refs, o_ref, scratch):
    qt_ref, kt_ref, vt_ref, sbias_ref, cnt_ref, rb0_ref, init_ref, ck_hbm, cv_hbm = in_refs
    kbuf, vbuf, sems = scratch
    n_seq = qt_ref.shape[1]
    units = SA_UNITS_PER_SEQ * n_host // n_steps
    assert units * n_steps == SA_UNITS_PER_SEQ * n_host and units % SA_SLOTS == 0
    heads = N_HEADS // SA_UNITS_PER_SEQ

    def seq_of(step, j):
        return seq_lo + step * (units // SA_UNITS_PER_SEQ) + j // SA_UNITS_PER_SEQ

    def copies(step, j):
        g, slot = j % SA_UNITS_PER_SEQ, j % SA_SLOTS
        return [pltpu.make_async_copy(hbm.at[layer, seq_of(step, j), pl.ds(g * heads, heads)],
                                      buf.at[slot], sems.at[n, slot])
                for n, (hbm, buf) in enumerate(((ck_hbm, kbuf), (cv_hbm, vbuf)))]

    def start(step, j):
        for cp in copies(step, j):
            cp.start()

    @pl.when(i == 0)
    def _():
        o_ref[...] = init_ref[...]
        for j in range(SA_SLOTS):
            start(i, j)

    lane = lax.broadcasted_iota(jnp.int32, (D_A, n_seq), 1)
    head = lambda a, h: a[h * HEAD_DIM:(h + 1) * HEAD_DIM]
    cols = {}

    def wait(j):
        for cp in copies(i, j):
            cp.wait()

    def refill(j):
        nxt = j + SA_SLOTS
        if nxt < units:
            start(i, nxt)
        else:
            @pl.when(i + 1 < n_steps)
            def _():
                start(i + 1, nxt - units)

    def unit_pieces(j):
        g, slot = j % SA_UNITS_PER_SEQ, j % SA_SLOTS
        h0 = g * heads
        st = {"rows": [], "new": [], "outs": []}

        def keys(hh_lo, hh_hi):
            if hh_lo == 0:
                st["s"] = seq_of(i, j)
                if j // SA_UNITS_PER_SEQ not in cols:
                    sel = lane == st["s"]
                    cols[j // SA_UNITS_PER_SEQ] = [
                        jnp.sum(jnp.where(sel, r[...], 0.0), axis=1, keepdims=True)
                        for r in (qt_ref, kt_ref, vt_ref)]
            qc, kc, _ = cols[j // SA_UNITS_PER_SEQ]
            for hh in range(hh_lo, hh_hi):
                st["rows"].append(jnp.sum(kbuf[slot, hh] * head(qc, h0 + hh), axis=0, keepdims=True))
                st["new"].append(jnp.sum(head(qc, h0 + hh) * head(kc, h0 + hh), axis=0,
                                         keepdims=True))

        def softmax():
            lg = jnp.concatenate(st["rows"], axis=0) + sbias_ref[h0:h0 + heads, :]
            lg0 = jnp.concatenate(st["new"], axis=0) + rb0_ref[h0:h0 + heads, :]
            m = jnp.maximum(jnp.max(lg, axis=1, keepdims=True), lg0)
            st["p"] = jnp.exp(lg - m) * cnt_ref[...]
            st["p0"] = len(DILATIONS) * jnp.exp(lg0 - m)
            st["den"] = jnp.sum(st["p"], axis=1, keepdims=True) + st["p0"]

        def values(hh_lo, hh_hi):
            vc = cols[j // SA_UNITS_PER_SEQ][2]
            for hh in range(hh_lo, hh_hi):
                num = jnp.sum(vbuf[slot, hh] * st["p"][hh:hh + 1, :], axis=1, keepdims=True)
                num = num + st["p0"][hh:hh + 1] * head(vc, h0 + hh)
                st["outs"].append(num / st["den"][hh:hh + 1])
            if hh_hi == heads:
                r0, r1 = h0 * HEAD_DIM, (h0 + heads) * HEAD_DIM
                mine = lax.broadcasted_iota(jnp.int32, (r1 - r0, n_seq), 1) == st["s"]
                o_ref[r0:r1, :] = jnp.where(mine, jnp.concatenate(st["outs"], axis=0),
                                            o_ref[r0:r1, :])
                refill(j)

        def whole():
            keys(0, heads)
            softmax()
            values(0, heads)

        return [functools.partial(wait, j), whole]

    return [unit_pieces(j) for j in range(units)]


def _prompt_mixer_pieces(tile, g_ref, c_ref, gx, cx, conv_dst, pool_dst, mixer_w, tm):
    cw_ref, cb_ref, lng_ref, lnb_ref, pw_ref, plw_ref, psc_ref = mixer_w
    ext = tm + 8
    state = {}

    def load():
        keep = (tile > 0).astype(F32)
        gx[0:HALO, :] = gx[0:HALO, :] * keep
        gx[HALO:HALO + tm, :] = g_ref[...]
        cx[0:HALO, :] = cx[0:HALO, :] * keep
        cx[HALO:HALO + tm, :] = c_ref[...]

    def taps(b_lo, b_hi):
        y = state.get("y")
        for b in range(b_lo, b_hi):
            part = None
            for a in range(-(-CONV_WIDTH // 8)):
                lag = 8 * a + b
                if lag >= CONV_WIDTH:
                    continue
                lo = HALO - 8 - 8 * a
                term = cw_ref[CONV_WIDTH - 1 - lag:CONV_WIDTH - lag, :] * gx[lo:lo + ext, :]
                part = term if part is None else part + term
            if b:
                part = pltpu.roll(part, b, 0)
            y = part if y is None else y + part
        state["y"] = y

    def conv_tail():
        conv_dst[...] = _conv_tail(state["y"][8:], cb_ref, lng_ref, lnb_ref, pw_ref).astype(BF16)

    def pool():
        x = c_ref[...]
        win = _pool_lane_consts((tm, D_C))
        run = cx[...]
        tot = jnp.zeros_like(x)
        prev_w = 1
        for w in POOL_WINDOWS:
            assert w == 2 * prev_w
            run = run + pltpu.roll(run, prev_w, 0)
            tot = jnp.where(win == w, run[HALO:], tot)
            prev_w = w
        pos = tile * tm + lax.broadcasted_iota(jnp.int32, (tm, D_C), 0)
        cnt = jnp.minimum(win, pos + 1).astype(F32)
        pool_dst[...] = _pool_tail(tot, x, cnt, plw_ref, psc_ref).astype(BF16)

    def first():
        load()
        taps(0, 1)

    return [first] + [functools.partial(taps, b, b + 1) for b in range(1, 8)] + [conv_tail, pool]


def _post_prompt_kernel(h_ref, attn_ref, g_ref, gh_ref, c_ref, ch_ref,
                        cw_ref, cb_ref, lng_ref, lnb_ref, pw_ref, plw_ref, psc_ref,
                        wout_ref, gf_ref, wgu_ref, wdn_ref, *rest,
                        tm, per_seq, layer, n_steps, sa_seqs):
    sa_in, outs_scratch, sa_scratch = rest[:N_SA_INPUTS], rest[N_SA_INPUTS:-3], rest[-3:]
    out_ref, sa_ref, gx, cx, conv_t, pool_t = outs_scratch
    mixer_w = (cw_ref, cb_ref, lng_ref, lnb_ref, pw_ref, plw_ref, psc_ref)
    i = pl.program_id(0)
    units = _sample_attn_units(i, n_steps, layer, *sa_seqs, sa_in, sa_ref, sa_scratch)
    gx[0:HALO, :] = gh_ref[...]
    cx[0:HALO, :] = ch_ref[...]
    for piece in _prompt_mixer_pieces(i % per_seq, g_ref, c_ref, gx, cx, conv_t, pool_t, mixer_w, tm):
        piece()
    out_ref[...] = _mix_ffn(h_ref[...], attn_ref[...], conv_t[...], pool_t[...],
                            wout_ref, gf_ref, wgu_ref, wdn_ref, _weave_plan(units))


def _post_sample_kernel(h_ref, attn_ref, g_ref, c_ref, cc_ref, cp_ref,
                        cw_ref, cb_ref, lng_ref, lnb_ref, pw_ref, plw_ref, psc_ref,
                        wout_ref, gf_ref, wgu_ref, wdn_ref, out_ref):
    n_hist = CONV_WIDTH - 1
    y = g_ref[...] * cw_ref[n_hist:n_hist + 1, :]
    for j in range(n_hist):
        y = y + cc_ref[0, j] * cw_ref[j:j + 1, :]
    conv = _conv_tail(y, cb_ref, lng_ref, lnb_ref, pw_ref)

    x = c_ref[...]
    win = _pool_lane_consts(x.shape)
    run = x
    tot = jnp.zeros_like(x)
    prev_w = 1
    for w in POOL_WINDOWS:
        for sft in range(prev_w, w):
            run = run + cp_ref[0, POOL_PREFIX - sft]
        tot = jnp.where(win == w, run, tot)
        prev_w = w
    pool = _pool_tail(tot, x, win.astype(F32), plw_ref, psc_ref)

    out_ref[...] = _mix_ffn(h_ref[...], attn_ref[...].T, conv, pool, wout_ref, gf_ref, wgu_ref, wdn_ref)


def _mixer_weight_specs(layer):
    return [_const_spec((CONV_WIDTH, D_B)), _const_spec((1, D_B)), _const_spec((1, D_B)),
            _const_spec((1, D_B)), _const_spec((D_B, D_B)), _const_spec((D_C, D_C)),
            _const_spec((1, D_C)),
            _layer_spec((D_MODEL, D_MODEL), layer), _const_spec((1, D_MODEL)),
            _layer_spec((D_MODEL, 2 * D_FF), layer), _layer_spec((D_FF, D_MODEL), layer)]


def _sample_attn_specs(sample):
    assert len(sample) == N_SA_INPUTS
    n_seq, wbuf = sample[0].shape[1], sample[-1].shape[-1]
    heads = N_HEADS // SA_UNITS_PER_SEQ
    cst = lambda shape: pl.BlockSpec(shape, lambda i: (0,) * len(shape))
    hbm = pl.BlockSpec(memory_space=pl.ANY)
    in_specs = [cst((D_A, n_seq))] * 3 + [cst((N_HEADS, wbuf)), cst((1, wbuf)), cst((N_HEADS, 1)),
                                          cst((D_A, n_seq)), hbm, hbm]
    scratch = [pltpu.VMEM((SA_SLOTS, heads, HEAD_DIM, wbuf), F32),
               pltpu.VMEM((SA_SLOTS, heads, HEAD_DIM, wbuf), F32),
               pltpu.SemaphoreType.DMA((2, SA_SLOTS))]
    return in_specs, cst((D_A, n_seq)), jax.ShapeDtypeStruct((D_A, n_seq), F32), scratch


def _post_prompt_call(layer, h, attn, glu, c, mixw, postw, sample, sa_seqs, tm, seq):
    t = h.shape[0]
    n_steps = t // tm
    per = tm // HALO
    row = lambda w: pl.BlockSpec((tm, w), lambda i: (i, 0))
    halo = lambda w: pl.BlockSpec((HALO, w), lambda i: (jnp.maximum(i * per - 1, 0), 0))
    sa_in, sa_out, sa_shape, sa_scratch = _sample_attn_specs(sample)
    return pl.pallas_call(
        functools.partial(_post_prompt_kernel, tm=tm, per_seq=seq // tm, layer=layer,
                          n_steps=n_steps, sa_seqs=sa_seqs),
        grid=(n_steps,),
        in_specs=[row(D_MODEL), row(D_A), row(D_B), halo(D_B), row(D_C), halo(D_C)]
        + _mixer_weight_specs(layer) + sa_in,
        out_specs=[row(D_MODEL), sa_out],
        out_shape=[jax.ShapeDtypeStruct((t, D_MODEL), F32), sa_shape],
        scratch_shapes=[pltpu.VMEM((HALO + tm, D_B), F32), pltpu.VMEM((HALO + tm, D_C), F32),
                        pltpu.VMEM((tm, D_B), BF16), pltpu.VMEM((tm, D_C), BF16)] + sa_scratch,
        compiler_params=pltpu.CompilerParams(dimension_semantics=("arbitrary",),
                                             vmem_limit_bytes=VMEM_LIMIT),
        name="post_prompt",
    )(h, attn, glu, glu, c, c, *mixw, *postw, *sample)


def _post_sample_call(layer, h, attn, glu, c, cct, cpt, mixw, postw):
    t = h.shape[0]
    full = lambda w: pl.BlockSpec((t, w), lambda i: (0, 0))
    return pl.pallas_call(
        _post_sample_kernel,
        grid=(1,),
        in_specs=[full(D_MODEL), pl.BlockSpec((D_A, t), lambda i: (0, 0)), full(D_B), full(D_C),
                  pl.BlockSpec((1, CONV_WIDTH - 1, t, D_B), lambda i: (layer, 0, 0, 0)),
                  pl.BlockSpec((1, POOL_PREFIX, t, D_C), lambda i: (layer, 0, 0, 0))]
        + _mixer_weight_specs(layer),
        out_specs=full(D_MODEL),
        out_shape=jax.ShapeDtypeStruct((t, D_MODEL), F32),
        compiler_params=pltpu.CompilerParams(dimension_semantics=("arbitrary",),
                                             vmem_limit_bytes=VMEM_LIMIT),
        name="post_sample",
    )(h, attn, glu, c, cct, cpt, *mixw, *postw)


def _bias_prompt_kernel(rb_ref, tbl_ref, out_ref):
    tbl = tbl_ref[0]
    for h in range(N_HEADS):
        def body(b, acc, h=h):
            return jnp.where(tbl == b, rb_ref[b, h], acc)

        acc = lax.fori_loop(0, NUM_BUCKETS, body, jnp.zeros(tbl.shape, F32))
        out_ref[0, h] = jnp.where(tbl < 0, NEG_INF, acc * LOG2E)


def _bias_prompt_call(rel_bias):
    tbl = jnp.asarray(_prompt_bucket_table())
    nbr = len(DILATIONS)
    return pl.pallas_call(
        _bias_prompt_kernel,
        grid=(nbr,),
        in_specs=[pl.BlockSpec(memory_space=pltpu.SMEM),
                  pl.BlockSpec((1, BLK, 2 * BLK), lambda br: (br, 0, 0))],
        out_specs=pl.BlockSpec((1, N_HEADS, BLK, 2 * BLK), lambda br: (br, 0, 0, 0)),
        out_shape=jax.ShapeDtypeStruct((nbr, N_HEADS, BLK, 2 * BLK), F32),
        name="bias_prompt",
    )(rel_bias, tbl)


def _bias_sample_kernel(rb_ref, tbl_ref, out_ref):
    tbl = tbl_ref[...]
    for h in range(N_HEADS):
        def body(b, acc, h=h):
            return jnp.where(tbl == b, rb_ref[b, h], acc)

        acc = lax.fori_loop(0, NUM_BUCKETS, body, jnp.zeros(tbl.shape, F32))
        out_ref[h:h + 1, :] = jnp.where(tbl < 0, NEG_INF, acc)


def _bias_sample_call(rel_bias, bucket):
    return pl.pallas_call(
        _bias_sample_kernel,
        in_specs=[pl.BlockSpec(memory_space=pltpu.SMEM),
                  pl.BlockSpec(bucket.shape, lambda: (0, 0))],
        out_shape=jax.ShapeDtypeStruct((N_HEADS, bucket.shape[1]), F32),
        name="bias_sample",
    )(rel_bias, bucket)


def _rows(start, size, dil):
    return pl.ds(start, size) if dil == 1 else pl.ds(start, size, stride=dil)


def _attn_prompt_kernel(q_ref, k_ref, v_ref, bias_ref, o_ref, m_acc, s_acc, n_acc, *, seq):
    lane_lo = lax.broadcasted_iota(jnp.int32, (BLK, 128), 1) < HEAD_DIM

    def group(br, dil, starts, n_keys, last):
        bias = bias_ref[br, 0, :, 2 * BLK - n_keys:]
        logits, values = [], []
        for q_start, k_start in starts:
            q = q_ref[0, _rows(q_start, BLK, dil), :] * LOG2E
            qs = jnp.concatenate([jnp.where(lane_lo, q, 0.0), jnp.where(lane_lo, 0.0, q)], axis=0)
            kb = k_ref[0, _rows(k_start, n_keys, dil), :].astype(BF16)
            values.append(v_ref[0, _rows(k_start, n_keys, dil), :].astype(BF16))
            logits.append(lax.dot_general(qs.astype(BF16), kb, (((1,), (1,)), ((), ())),
                                          preferred_element_type=F32) + bias)
        stats = []
        for lg, vb in zip(logits, values):
            m = jnp.max(lg, axis=-1, keepdims=True)
            p = jnp.exp2(lg - m)
            s = jnp.sum(p, axis=-1, keepdims=True)
            pv = _dot(p.astype(BF16), vb)
            stats.append((jnp.where(lane_lo, m[:BLK], m[BLK:]),
                          jnp.where(lane_lo, s[:BLK], s[BLK:]),
                          jnp.where(lane_lo, pv[:BLK], pv[BLK:])))
        rows = [_rows(q_start, BLK, dil) for q_start, _ in starts]
        if br == 0:
            for r, (mm, ss, num) in zip(rows, stats):
                m_acc[r, :] = mm
                s_acc[r, :] = ss
                n_acc[r, :] = num
            return
        old = [(m_acc[r, :], s_acc[r, :], n_acc[r, :]) for r in rows]
        for r, (mm, ss, num), (m_old, s_old, n_old) in zip(rows, stats, old):
            m_new = jnp.maximum(m_old, mm)
            a = jnp.exp2(m_old - m_new)
            b = jnp.exp2(mm - m_new)
            s_new = a * s_old + b * ss
            n_new = a * n_old + b * num
            if last:
                o_ref[0, r, :] = n_new / s_new
            else:
                m_acc[r, :] = m_new
                s_acc[r, :] = s_new
                n_acc[r, :] = n_new

    for br, dil in enumerate(DILATIONS):
        last = br == len(DILATIONS) - 1
        nb = seq // dil // BLK

        def starts_of(u, first, dil=dil):
            if first:
                return u, u
            lb = u // dil + 1
            r = u % dil
            return r + dil * BLK * lb, r + dil * BLK * (lb - 1)

        for first, n_units in ((True, dil), (False, dil * (nb - 1))):
            n_keys = BLK if first else 2 * BLK
            rem = n_units % ATTN_GROUP
            if rem:
                group(br, dil, [starts_of(u, first) for u in range(rem)], n_keys, last)

            def body(g, carry, br=br, dil=dil, last=last, first=first, rem=rem, n_keys=n_keys,
                     starts_of=starts_of):
                u0 = rem + g * ATTN_GROUP
                group(br, dil, [starts_of(u0 + j, first) for j in range(ATTN_GROUP)], n_keys, last)
                return carry

            lax.fori_loop(0, n_units // ATTN_GROUP, body, 0)


def _attn_prompt_call(q, k, v, bias2):
    bsz, seq, _ = q.shape
    blk = pl.BlockSpec((1, seq, 128), lambda b, hp: (b, 0, hp))
    return pl.pallas_call(
        functools.partial(_attn_prompt_kernel, seq=seq),
        grid=(bsz, N_HEAD_PAIRS),
        in_specs=[blk, blk, blk,
                  pl.BlockSpec((len(DILATIONS), 1, HEADS_PER_VREG * BLK, 2 * BLK),
                               lambda b, hp: (0, hp, 0, 0))],
        out_specs=blk,
        out_shape=jax.ShapeDtypeStruct((bsz, seq, D_A), F32),
        scratch_shapes=[pltpu.VMEM((seq, 128), F32)] * 3,
        compiler_params=pltpu.CompilerParams(dimension_semantics=("arbitrary", "arbitrary"),
                                             vmem_limit_bytes=VMEM_LIMIT),
        name="attn_prompt",
    )(q, k, v, bias2)


def _block_diag(blocks):
    n, r, c = blocks.shape
    eye = jnp.eye(n, dtype=blocks.dtype)
    return (eye[:, None, :, None] * blocks[:, :, None, :]).reshape(n * r, n * c)


def kernel(x_prompt, x_sample, cache_attn_k, cache_attn_v, cache_conv, cache_pool, rel_bias, g_ffn1, w_ffn1_gu, w_ffn1_down, g_mix, w_in, g_q, g_k, conv_w, conv_b, conv_ln_g, conv_ln_b, conv_pw, pool_w, pool_scale, w_out, g_ffn2, w_ffn2_gu, w_ffn2_down):
    bsz, seq, _ = x_prompt.shape
    nseq = x_sample.shape[0]
    depth = g_ffn1.shape[0]
    wbuf = cache_attn_k.shape[2]
    assert x_sample.shape[1] == 1 and wbuf == DILATIONS[-1] * SPAN and nseq == 128
    tm_prompt, tm_post, tm_sample = 512, 512, nseq

    head_of_lane = np.arange(D_A) // HEAD_DIM
    bd = jnp.asarray((head_of_lane[:, None] == head_of_lane[None, :]) / HEAD_DIM, BF16)
    bias_p = _bias_prompt_call(rel_bias).reshape(
        len(DILATIONS), N_HEAD_PAIRS, HEADS_PER_VREG * BLK, 2 * BLK)
    s_bucket, s_count = _sample_tables(wbuf)
    bias_s = _bias_sample_call(rel_bias, jnp.asarray(s_bucket))
    s_count = jnp.asarray(s_count)
    rb0 = rel_bias[0].reshape(N_HEADS, 1)
    ck = jnp.transpose(cache_attn_k, (0, 1, 3, 4, 2))
    cv = jnp.transpose(cache_attn_v, (0, 1, 3, 4, 2))
    cct = jnp.transpose(cache_conv, (0, 2, 1, 3))
    cpt = jnp.transpose(cache_pool, (0, 2, 1, 3))

    hp = x_prompt.reshape(bsz * seq, D_MODEL)
    hs = x_sample.reshape(nseq, D_MODEL)
    outs = {n: [] for n in ("pk", "pv", "pc", "pp", "sk", "sv", "sc", "sp")}
    wgu1, wdn1, win = (w.astype(BF16) for w in (w_ffn1_gu, w_ffn1_down, w_in))
    wgu2, wdn2, wout = (w.astype(BF16) for w in (w_ffn2_gu, w_ffn2_down, w_out))
    for l in range(depth):
        r1 = lambda a: a[l].reshape(1, -1)
        gq = jnp.tile(g_q[l], N_HEADS).reshape(1, D_A)
        gk = jnp.tile(g_k[l], N_HEADS).reshape(1, D_A)
        pw = conv_pw[l].astype(BF16)
        plw = _block_diag(pool_w[l]).astype(BF16)
        mixw = (conv_w[l], r1(conv_b), r1(conv_ln_g), r1(conv_ln_b), pw, plw, r1(pool_scale))
        pre_w = (r1(g_ffn1), wgu1, wdn1, r1(g_mix), win, gq, gk, bd)
        post_w = (wout, r1(g_ffn2), wgu2, wdn2)

        keep = min(wbuf, seq)
        hs, _, sk, sv, sglu, sc, sq_t, sk_t, sv_t = _pre_call(l, hs, *pre_w, tm=tm_sample)
        hp, q, k, v, glu, c, k_t, v_t = _pre_call(l, hp, *pre_w, tm=tm_prompt, seq=seq, keep=keep)

        sq = lambda a: a.reshape(bsz, seq, a.shape[-1])
        attn = _attn_prompt_call(sq(q), sq(k), sq(v), bias_p)
        sample = (sq_t, sk_t, sv_t, bias_s, s_count, rb0, jnp.zeros((D_A, nseq), F32), ck, cv)
        hp, sattn_t = _post_prompt_call(
            l, hp, attn.reshape(bsz * seq, D_A), glu, c, mixw, post_w,
            sample, (0, nseq), tm=tm_post, seq=seq)
        to_cache = lambda a: a.reshape(bsz, N_HEADS, HEAD_DIM, keep).transpose(0, 3, 1, 2)
        outs["pk"].append(to_cache(k_t))
        outs["pv"].append(to_cache(v_t))
        outs["pc"].append(sq(glu)[:, seq - (CONV_WIDTH - 1):])
        outs["pp"].append(sq(c)[:, seq - POOL_PREFIX:])

        hs = _post_sample_call(l, hs, sattn_t, sglu, sc, cct, cpt, mixw, post_w)
        outs["sk"].append(sk.reshape(nseq, 1, N_HEADS, HEAD_DIM))
        outs["sv"].append(sv.reshape(nseq, 1, N_HEADS, HEAD_DIM))
        outs["sc"].append(jnp.concatenate([cache_conv[l][:, 1:], sglu[:, None, :]], axis=1))
        outs["sp"].append(jnp.concatenate([cache_pool[l][:, 1:], sc[:, None, :]], axis=1))

    st = lambda n: jnp.stack(outs[n])
    return (hp.reshape(bsz, seq, D_MODEL), hs.reshape(nseq, 1, D_MODEL),
            st("pk"), st("pv"), st("pc"), st("pp"), st("sk"), st("sv"), st("sc"), st("sp"))
```

```python
import functools
import math

import numpy as np
import jax
import jax.numpy as jnp
from jax import lax
from jax.experimental import pallas as pl
from jax.experimental.pallas import tpu as pltpu

F32 = jnp.float32
BF16 = jnp.bfloat16

D_MODEL = 1024
HEAD_DIM = 64
N_HEADS = 8
D_A = N_HEADS * HEAD_DIM
D_B = 256
D_C = 256
D_IN = 3 * D_A + 2 * D_B + D_C
D_FF = 2816
DILATIONS = (1, 4, 16)
SPAN = 128
BLK = 128
CONV_WIDTH = 31
POOL_WINDOWS = (2, 4, 8, 16)
POOL_GROUP = 64
POOL_PREFIX = 15
NUM_BUCKETS = 32
MAX_EXACT = 16
REL_MAX_DIST = 2048
EPS = 1e-6
NEG_INF = -1e30

HEADS_PER_VREG = 128 // HEAD_DIM
N_HEAD_PAIRS = N_HEADS // HEADS_PER_VREG
MXU_WIDTH = 256
HALO = 32
FF_CHUNK = 256
ATTN_GROUP = 8
SA_UNITS_PER_SEQ = 1
SA_SLOTS = 2
LOG2E = math.log2(math.e)
VMEM_LIMIT = 60 * 1024 * 1024


def _bucket_np(dist):
    dist = np.asarray(dist, np.int64)
    ratio = np.log(np.maximum(dist, 1).astype(np.float32) / np.float32(MAX_EXACT))
    large = MAX_EXACT + (ratio / np.float32(math.log(REL_MAX_DIST / MAX_EXACT))
                         * np.float32(NUM_BUCKETS - MAX_EXACT)).astype(np.int32)
    return np.where(dist < MAX_EXACT, dist, np.minimum(large, NUM_BUCKETS - 1)).astype(np.int32)


def _prompt_bucket_table():
    qi = np.arange(BLK)[:, None]
    kk = np.arange(2 * BLK)[None, :]
    dist = BLK + qi - kk
    ok = (dist >= 0) & (dist <= SPAN)
    tabs = []
    for dil in DILATIONS:
        b = _bucket_np(dil * np.clip(dist, 0, SPAN))
        tabs.append(np.where(ok, b, -1))
    return np.stack(tabs).astype(np.int32)


def _sample_tables(wbuf):
    dist = wbuf - np.arange(wbuf)
    count = np.zeros(wbuf, np.int32)
    for dil in DILATIONS:
        count += ((dist % dil == 0) & (dist // dil <= SPAN)).astype(np.int32)
    bucket = np.where(count > 0, _bucket_np(dist), -1).astype(np.int32)
    return bucket[None, :], count.astype(np.float32)[None, :]


def _dot(a, b):
    return jnp.dot(a, b, preferred_element_type=F32)


def _rms(x, g):
    ms = jnp.mean(x * x, axis=-1, keepdims=True)
    return x * lax.rsqrt(ms + EPS) * g


def _silu(x):
    return x * jax.nn.sigmoid(x)


FFN_WEAVE_SLOTS = 3 * (D_FF // FF_CHUNK)


def _weave_plan(units):
    plan = {}
    for n, (wait, work) in enumerate(units):
        chunk = (n * (FFN_WEAVE_SLOTS // 3)) // len(units)
        assert 3 * chunk - 1 not in plan
        plan[3 * chunk - 1] = wait
        plan[3 * chunk + 2] = work
    return plan


def _swiglu_residual(h, g_ref, wgu_ref, wdn_ref, weave=None):
    weave = weave or {}
    xn = _rms(h, g_ref[...]).astype(BF16)
    if -1 in weave:
        weave[-1]()
    acc = None
    for c in range(D_FF // FF_CHUNK):
        lo = c * FF_CHUNK

        def after(n, c=c):
            if 3 * c + n in weave:
                weave[3 * c + n]()

        g = _dot(xn, wgu_ref[:, lo:lo + FF_CHUNK])
        after(0)
        u = _dot(xn, wgu_ref[:, D_FF + lo:D_FF + lo + FF_CHUNK])
        after(1)
        a = (_silu(g) * u).astype(BF16)
        d = _dot(a, wdn_ref[lo:lo + FF_CHUNK, :])
        acc = d if acc is None else acc + d
        after(2)
    return h + 0.5 * acc


def _head_rms(x, bd_ref, g):
    w = bd_ref.shape[0]
    x2 = (x * x).astype(BF16)
    ms = jnp.concatenate([_dot(x2[:, lo:lo + w], bd_ref[...]) for lo in range(0, D_A, w)], axis=1)
    return x * lax.rsqrt(ms + EPS) * g


def _layernorm(x, g, b):
    mu = jnp.mean(x, axis=-1, keepdims=True)
    xc = x - mu
    var = jnp.mean(xc * xc, axis=-1, keepdims=True)
    return xc * lax.rsqrt(var + EPS) * g + b


def _pre_kernel(h_ref, gf_ref, wgu_ref, wdn_ref, gm_ref, win_ref, gq_ref, gk_ref, bd_ref, *rest,
                n_t, sa=None):
    weave = None
    if sa is not None:
        sa_in, rest, sa_scratch = rest[:N_SA_INPUTS], rest[N_SA_INPUTS:-3], rest[-3:]
        rest, sa_ref = rest[:-1], rest[-1]
        layer, n_steps, seq_lo, n_host = sa
        weave = _weave_plan(_sample_attn_units(pl.program_id(0), n_steps, layer, seq_lo, n_host,
                                               sa_in, sa_ref, sa_scratch))
    h1_ref, q_ref, k_ref, v_ref, glu_ref, c_ref = rest[:6]
    t_refs = rest[6:]
    assert len(t_refs) == n_t
    h1 = _swiglu_residual(h_ref[...], gf_ref, wgu_ref, wdn_ref, weave)
    h1_ref[...] = h1
    xn = _rms(h1, gm_ref[...]).astype(BF16)
    q = _head_rms(_dot(xn, win_ref[:, 0:D_A]), bd_ref, gq_ref[...]) * (HEAD_DIM ** -0.5)
    k = _head_rms(_dot(xn, win_ref[:, D_A:2 * D_A]), bd_ref, gk_ref[...])
    v = _dot(xn, win_ref[:, 2 * D_A:3 * D_A])
    q_ref[...] = q
    k_ref[...] = k
    v_ref[...] = v
    for t_ref, val in zip(t_refs, (q, k, v)[3 - len(t_refs):]):
        t_ref[...] = val.T.reshape(t_ref.shape)
    o = 3 * D_A
    b_val = _dot(xn, win_ref[:, o:o + D_B])
    b_gate = _dot(xn, win_ref[:, o + D_B:o + 2 * D_B])
    glu_ref[...] = b_val * jax.nn.sigmoid(b_gate)
    c_ref[...] = _dot(xn, win_ref[:, o + 2 * D_B:o + 2 * D_B + D_C])


def _const_spec(shape):
    nd = len(shape)
    return pl.BlockSpec(shape, lambda i, _nd=nd: (0,) * _nd, pipeline_mode=pl.Buffered(1))


def _layer_spec(shape, layer):
    nd = len(shape)
    return pl.BlockSpec((None,) + tuple(shape), lambda i, _nd=nd: (layer,) + (0,) * _nd,
                        pipeline_mode=pl.Buffered(1))


def _pre_call(layer, h, gf, wgu, wdn, gm, win, gq, gk, bd, tm, seq=None, keep=None, sample=None,
              sa_seqs=None):
    t = h.shape[0]
    n_steps = t // tm
    row = lambda w: pl.BlockSpec((tm, w), lambda i: (i, 0))
    in_specs = [row(D_MODEL), _const_spec((1, D_MODEL)), _layer_spec((D_MODEL, 2 * D_FF), layer),
                _layer_spec((D_FF, D_MODEL), layer), _const_spec((1, D_MODEL)),
                _layer_spec((D_MODEL, D_IN), layer), _const_spec((1, D_A)), _const_spec((1, D_A)),
                _const_spec((MXU_WIDTH, MXU_WIDTH))]
    out_specs = [row(D_MODEL), row(D_A), row(D_A), row(D_A), row(D_B), row(D_C)]
    out_shape = [jax.ShapeDtypeStruct((t, w), F32) for w in (D_MODEL, D_A, D_A, D_A, D_B, D_C)]
    if seq is not None:
        per_seq, skip = seq // tm, (seq - keep) // tm
        t_spec = pl.BlockSpec((1, D_A, tm),
                              lambda i: (i // per_seq, 0, jnp.maximum(i % per_seq - skip, 0)))
        out_specs += [t_spec] * 2
        out_shape += [jax.ShapeDtypeStruct((t // seq, D_A, keep), F32)] * 2
    else:
        out_specs += [pl.BlockSpec((D_A, tm), lambda i: (0, i))] * 3
        out_shape += [jax.ShapeDtypeStruct((D_A, t), F32)] * 3
    n_t = len(out_specs) - 6
    scratch, sa, operands = [], None, ()
    if sample is not None:
        sa_in, sa_out, sa_shape, scratch = _sample_attn_specs(sample)
        in_specs += sa_in
        out_specs.append(sa_out)
        out_shape.append(sa_shape)
        sa, operands = (layer, n_steps) + tuple(sa_seqs), tuple(sample)
    return pl.pallas_call(
        functools.partial(_pre_kernel, n_t=n_t, sa=sa),
        grid=(n_steps,),
        in_specs=in_specs,
        out_specs=out_specs,
        out_shape=out_shape,
        scratch_shapes=scratch,
        compiler_params=pltpu.CompilerParams(dimension_semantics=("arbitrary",),
                                             vmem_limit_bytes=VMEM_LIMIT),
        name="pre_ffn_inproj",
    )(h, gf, wgu, wdn, gm, win, gq, gk, bd, *operands)


def _pool_lane_consts(shape):
    grp = lax.broadcasted_iota(jnp.int32, shape, len(shape) - 1) // POOL_GROUP
    win = jnp.full(shape, POOL_WINDOWS[0], jnp.int32)
    for g in range(1, len(POOL_WINDOWS)):
        win = jnp.where(grp == g, POOL_WINDOWS[g], win)
    return win


def _conv_tail(y, cb_ref, lng_ref, lnb_ref, pw_ref):
    y = _layernorm(y + cb_ref[...], lng_ref[...], lnb_ref[...])
    return _dot(_silu(y).astype(BF16), pw_ref[...])


def _pool_tail(tot, x, cnt, plw_ref, psc_ref):
    d = tot / cnt - x
    return _dot(d.astype(BF16), plw_ref[...]) * psc_ref[...]


def _mix_ffn(h, attn, conv, pool, wout_ref, gf_ref, wgu_ref, wdn_ref, weave=None):
    mix = _dot(attn.astype(BF16), wout_ref[0:D_A, :])
    mix = mix + _dot(conv.astype(BF16), wout_ref[D_A:D_A + D_B, :])
    mix = mix + _dot(pool.astype(BF16), wout_ref[D_A + D_B:D_A + D_B + D_C, :])
    return _swiglu_residual(h + mix, gf_ref, wgu_ref, wdn_ref, weave)


N_SA_INPUTS = 9


def _sample_attn_units(i, n_steps, layer, seq_lo, n_host, in_refs, o_ref, scratch):
    qt_ref, kt_ref, vt_ref, sbias_ref, cnt_ref, rb0_ref, init_ref, ck_hbm, cv_hbm = in_refs
    kbuf, vbuf, sems = scratch
    n_seq = qt_ref.shape[1]
    units = SA_UNITS_PER_SEQ * n_host // n_steps
    assert units * n_steps == SA_UNITS_PER_SEQ * n_host and units % SA_SLOTS == 0
    heads = N_HEADS // SA_UNITS_PER_SEQ

    def seq_of(step, j):
        return seq_lo + step * (units // SA_UNITS_PER_SEQ) + j // SA_UNITS_PER_SEQ

    def copies(step, j):
        g, slot = j % SA_UNITS_PER_SEQ, j % SA_SLOTS
        return [pltpu.make_async_copy(hbm.at[layer, seq_of(step, j), pl.ds(g * heads, heads)],
                                      buf.at[slot], sems.at[n, slot])
                for n, (hbm, buf) in enumerate(((ck_hbm, kbuf), (cv_hbm, vbuf)))]

    def start(step, j):
        for cp in copies(step, j):
            cp.start()

    @pl.when(i == 0)
    def _():
        o_ref[...] = init_ref[...]
        for j in range(SA_SLOTS):
            start(i, j)

    lane = lax.broadcasted_iota(jnp.int32, (D_A, n_seq), 1)
    head = lambda a, h: a[h * HEAD_DIM:(h + 1) * HEAD_DIM]
    cols = {}

    def wait(j):
        for cp in copies(i, j):
            cp.wait()

    def refill(j):
        nxt = j + SA_SLOTS
        if nxt < units:
            start(i, nxt)
        else:
            @pl.when(i + 1 < n_steps)
            def _():
                start(i + 1, nxt - units)

    def unit_pieces(j):
        g, slot = j % SA_UNITS_PER_SEQ, j % SA_SLOTS
        h0 = g * heads
        st = {"rows": [], "new": [], "outs": []}

        def keys(hh_lo, hh_hi):
            if hh_lo == 0:
                st["s"] = seq_of(i, j)
                if j // SA_UNITS_PER_SEQ not in cols:
                    sel = lane == st["s"]
                    cols[j // SA_UNITS_PER_SEQ] = [
                        jnp.sum(jnp.where(sel, r[...], 0.0), axis=1, keepdims=True)
                        for r in (qt_ref, kt_ref, vt_ref)]
            qc, kc, _ = cols[j // SA_UNITS_PER_SEQ]
            for hh in range(hh_lo, hh_hi):
                st["rows"].append(jnp.sum(kbuf[slot, hh] * head(qc, h0 + hh), axis=0, keepdims=True))
                st["new"].append(jnp.sum(head(qc, h0 + hh) * head(kc, h0 + hh), axis=0,
                                         keepdims=True))

        def softmax():
            lg = jnp.concatenate(st["rows"], axis=0) + sbias_ref[h0:h0 + heads, :]
            lg0 = jnp.concatenate(st["new"], axis=0) + rb0_ref[h0:h0 + heads, :]
            m = jnp.maximum(jnp.max(lg, axis=1, keepdims=True), lg0)
            st["p"] = jnp.exp(lg - m) * cnt_ref[...]
            st["p0"] = len(DILATIONS) * jnp.exp(lg0 - m)
            st["den"] = jnp.sum(st["p"], axis=1, keepdims=True) + st["p0"]

        def values(hh_lo, hh_hi):
            vc = cols[j // SA_UNITS_PER_SEQ][2]
            for hh in range(hh_lo, hh_hi):
                num = jnp.sum(vbuf[slot, hh] * st["p"][hh:hh + 1, :], axis=1, keepdims=True)
                num = num + st["p0"][hh:hh + 1] * head(vc, h0 + hh)
                st["outs"].append(num / st["den"][hh:hh + 1])
            if hh_hi == heads:
                r0, r1 = h0 * HEAD_DIM, (h0 + heads) * HEAD_DIM
                mine = lax.broadcasted_iota(jnp.int32, (r1 - r0, n_seq), 1) == st["s"]
                o_ref[r0:r1, :] = jnp.where(mine, jnp.concatenate(st["outs"], axis=0),
                                            o_ref[r0:r1, :])
                refill(j)

        def whole():
            keys(0, heads)
            softmax()
            values(0, heads)

        return [functools.partial(wait, j), whole]

    return [unit_pieces(j) for j in range(units)]


def _prompt_mixer_pieces(tile, g_ref, c_ref, gx, cx, conv_dst, pool_dst, mixer_w, tm):
    cw_ref, cb_ref, lng_ref, lnb_ref, pw_ref, plw_ref, psc_ref = mixer_w
    ext = tm + 8
    state = {}

    def load():
        keep = (tile > 0).astype(F32)
        gx[0:HALO, :] = gx[0:HALO, :] * keep
        gx[HALO:HALO + tm, :] = g_ref[...]
        cx[0:HALO, :] = cx[0:HALO, :] * keep
        cx[HALO:HALO + tm, :] = c_ref[...]

    def taps(b_lo, b_hi):
        y = state.get("y")
        for b in range(b_lo, b_hi):
            part = None
            for a in range(-(-CONV_WIDTH // 8)):
                lag = 8 * a + b
                if lag >= CONV_WIDTH:
                    continue
                lo = HALO - 8 - 8 * a
                term = cw_ref[CONV_WIDTH - 1 - lag:CONV_WIDTH - lag, :] * gx[lo:lo + ext, :]
                part = term if part is None else part + term
            if b:
                part = pltpu.roll(part, b, 0)
            y = part if y is None else y + part
        state["y"] = y

    def conv_tail():
        conv_dst[...] = _conv_tail(state["y"][8:], cb_ref, lng_ref, lnb_ref, pw_ref).astype(BF16)

    def pool():
        x = c_ref[...]
        win = _pool_lane_consts((tm, D_C))
        run = cx[...]
        tot = jnp.zeros_like(x)
        prev_w = 1
        for w in POOL_WINDOWS:
            assert w == 2 * prev_w
            run = run + pltpu.roll(run, prev_w, 0)
            tot = jnp.where(win == w, run[HALO:], tot)
            prev_w = w
        pos = tile * tm + lax.broadcasted_iota(jnp.int32, (tm, D_C), 0)
        cnt = jnp.minimum(win, pos + 1).astype(F32)
        pool_dst[...] = _pool_tail(tot, x, cnt, plw_ref, psc_ref).astype(BF16)

    def first():
        load()
        taps(0, 1)

    return [first] + [functools.partial(taps, b, b + 1) for b in range(1, 8)] + [conv_tail, pool]


def _post_prompt_kernel(h_ref, attn_ref, g_ref, gh_ref, c_ref, ch_ref,
                        cw_ref, cb_ref, lng_ref, lnb_ref, pw_ref, plw_ref, psc_ref,
                        wout_ref, gf_ref, wgu_ref, wdn_ref, *rest,
                        tm, per_seq, layer, n_steps, sa_seqs):
    sa_in, outs_scratch, sa_scratch = rest[:N_SA_INPUTS], rest[N_SA_INPUTS:-3], rest[-3:]
    out_ref, sa_ref, gx, cx, conv_t, pool_t = outs_scratch
    mixer_w = (cw_ref, cb_ref, lng_ref, lnb_ref, pw_ref, plw_ref, psc_ref)
    i = pl.program_id(0)
    units = _sample_attn_units(i, n_steps, layer, *sa_seqs, sa_in, sa_ref, sa_scratch)
    gx[0:HALO, :] = gh_ref[...]
    cx[0:HALO, :] = ch_ref[...]
    for piece in _prompt_mixer_pieces(i % per_seq, g_ref, c_ref, gx, cx, conv_t, pool_t, mixer_w, tm):
        piece()
    out_ref[...] = _mix_ffn(h_ref[...], attn_ref[...], conv_t[...], pool_t[...],
                            wout_ref, gf_ref, wgu_ref, wdn_ref, _weave_plan(units))


def _post_sample_kernel(h_ref, attn_ref, g_ref, c_ref, cc_ref, cp_ref,
                        cw_ref, cb_ref, lng_ref, lnb_ref, pw_ref, plw_ref, psc_ref,
                        wout_ref, gf_ref, wgu_ref, wdn_ref, out_ref):
    n_hist = CONV_WIDTH - 1
    y = g_ref[...] * cw_ref[n_hist:n_hist + 1, :]
    for j in range(n_hist):
        y = y + cc_ref[0, j] * cw_ref[j:j + 1, :]
    conv = _conv_tail(y, cb_ref, lng_ref, lnb_ref, pw_ref)

    x = c_ref[...]
    win = _pool_lane_consts(x.shape)
    run = x
    tot = jnp.zeros_like(x)
    prev_w = 1
    for w in POOL_WINDOWS:
        for sft in range(prev_w, w):
            run = run + cp_ref[0, POOL_PREFIX - sft]
        tot = jnp.where(win == w, run, tot)
        prev_w = w
    pool = _pool_tail(tot, x, win.astype(F32), plw_ref, psc_ref)

    out_ref[...] = _mix_ffn(h_ref[...], attn_ref[...].T, conv, pool, wout_ref, gf_ref, wgu_ref, wdn_ref)


def _mixer_weight_specs(layer):
    return [_const_spec((CONV_WIDTH, D_B)), _const_spec((1, D_B)), _const_spec((1, D_B)),
            _const_spec((1, D_B)), _const_spec((D_B, D_B)), _const_spec((D_C, D_C)),
            _const_spec((1, D_C)),
            _layer_spec((D_MODEL, D_MODEL), layer), _const_spec((1, D_MODEL)),
            _layer_spec((D_MODEL, 2 * D_FF), layer), _layer_spec((D_FF, D_MODEL), layer)]


def _sample_attn_specs(sample):
    assert len(sample) == N_SA_INPUTS
    n_seq, wbuf = sample[0].shape[1], sample[-1].shape[-1]
    heads = N_HEADS // SA_UNITS_PER_SEQ
    cst = lambda shape: pl.BlockSpec(shape, lambda i: (0,) * len(shape))
    hbm = pl.BlockSpec(memory_space=pl.ANY)
    in_specs = [cst((D_A, n_seq))] * 3 + [cst((N_HEADS, wbuf)), cst((1, wbuf)), cst((N_HEADS, 1)),
                                          cst((D_A, n_seq)), hbm, hbm]
    scratch = [pltpu.VMEM((SA_SLOTS, heads, HEAD_DIM, wbuf), F32),
               pltpu.VMEM((SA_SLOTS, heads, HEAD_DIM, wbuf), F32),
               pltpu.SemaphoreType.DMA((2, SA_SLOTS))]
    return in_specs, cst((D_A, n_seq)), jax.ShapeDtypeStruct((D_A, n_seq), F32), scratch


def _post_prompt_call(layer, h, attn, glu, c, mixw, postw, sample, sa_seqs, tm, seq):
    t = h.shape[0]
    n_steps = t // tm
    per = tm // HALO
    row = lambda w: pl.BlockSpec((tm, w), lambda i: (i, 0))
    halo = lambda w: pl.BlockSpec((HALO, w), lambda i: (jnp.maximum(i * per - 1, 0), 0))
    sa_in, sa_out, sa_shape, sa_scratch = _sample_attn_specs(sample)
    return pl.pallas_call(
        functools.partial(_post_prompt_kernel, tm=tm, per_seq=seq // tm, layer=layer,
                          n_steps=n_steps, sa_seqs=sa_seqs),
        grid=(n_steps,),
        in_specs=[row(D_MODEL), row(D_A), row(D_B), halo(D_B), row(D_C), halo(D_C)]
        + _mixer_weight_specs(layer) + sa_in,
        out_specs=[row(D_MODEL), sa_out],
        out_shape=[jax.ShapeDtypeStruct((t, D_MODEL), F32), sa_shape],
        scratch_shapes=[pltpu.VMEM((HALO + tm, D_B), F32), pltpu.VMEM((HALO + tm, D_C), F32),
                        pltpu.VMEM((tm, D_B), BF16), pltpu.VMEM((tm, D_C), BF16)] + sa_scratch,
        compiler_params=pltpu.CompilerParams(dimension_semantics=("arbitrary",),
                                             vmem_limit_bytes=VMEM_LIMIT),
        name="post_prompt",
    )(h, attn, glu, glu, c, c, *mixw, *postw, *sample)


def _post_sample_call(layer, h, attn, glu, c, cct, cpt, mixw, postw):
    t = h.shape[0]
    full = lambda w: pl.BlockSpec((t, w), lambda i: (0, 0))
    return pl.pallas_call(
        _post_sample_kernel,
        grid=(1,),
        in_specs=[full(D_MODEL), pl.BlockSpec((D_A, t), lambda i: (0, 0)), full(D_B), full(D_C),
                  pl.BlockSpec((1, CONV_WIDTH - 1, t, D_B), lambda i: (layer, 0, 0, 0)),
                  pl.BlockSpec((1, POOL_PREFIX, t, D_C), lambda i: (layer, 0, 0, 0))]
        + _mixer_weight_specs(layer),
        out_specs=full(D_MODEL),
        out_shape=jax.ShapeDtypeStruct((t, D_MODEL), F32),
        compiler_params=pltpu.CompilerParams(dimension_semantics=("arbitrary",),
                                             vmem_limit_bytes=VMEM_LIMIT),
        name="post_sample",
    )(h, attn, glu, c, cct, cpt, *mixw, *postw)


def _bias_prompt_kernel(rb_ref, tbl_ref, out_ref):
    tbl = tbl_ref[0]
    for h in range(N_HEADS):
        def body(b, acc, h=h):
            return jnp.where(tbl == b, rb_ref[b, h], acc)

        acc = lax.fori_loop(0, NUM_BUCKETS, body, jnp.zeros(tbl.shape, F32))
        out_ref[0, h] = jnp.where(tbl < 0, NEG_INF, acc * LOG2E)


def _bias_prompt_call(rel_bias):
    tbl = jnp.asarray(_prompt_bucket_table())
    nbr = len(DILATIONS)
    return pl.pallas_call(
        _bias_prompt_kernel,
        grid=(nbr,),
        in_specs=[pl.BlockSpec(memory_space=pltpu.SMEM),
                  pl.BlockSpec((1, BLK, 2 * BLK), lambda br: (br, 0, 0))],
        out_specs=pl.BlockSpec((1, N_HEADS, BLK, 2 * BLK), lambda br: (br, 0, 0, 0)),
        out_shape=jax.ShapeDtypeStruct((nbr, N_HEADS, BLK, 2 * BLK), F32),
        name="bias_prompt",
    )(rel_bias, tbl)


def _bias_sample_kernel(rb_ref, tbl_ref, out_ref):
    tbl = tbl_ref[...]
    for h in range(N_HEADS):
        def body(b, acc, h=h):
            return jnp.where(tbl == b, rb_ref[b, h], acc)

        acc = lax.fori_loop(0, NUM_BUCKETS, body, jnp.zeros(tbl.shape, F32))
        out_ref[h:h + 1, :] = jnp.where(tbl < 0, NEG_INF, acc)


def _bias_sample_call(rel_bias, bucket):
    return pl.pallas_call(
        _bias_sample_kernel,
        in_specs=[pl.BlockSpec(memory_space=pltpu.SMEM),
                  pl.BlockSpec(bucket.shape, lambda: (0, 0))],
        out_shape=jax.ShapeDtypeStruct((N_HEADS, bucket.shape[1]), F32),
        name="bias_sample",
    )(rel_bias, bucket)


def _rows(start, size, dil):
    return pl.ds(start, size) if dil == 1 else pl.ds(start, size, stride=dil)


def _attn_prompt_kernel(q_ref, k_ref, v_ref, bias_ref, o_ref, m_acc, s_acc, n_acc, *, seq):
    lane_lo = lax.broadcasted_iota(jnp.int32, (BLK, 128), 1) < HEAD_DIM

    def group(br, dil, starts, n_keys, order):
        bias = bias_ref[br, 0, :, 2 * BLK - n_keys:]
        logits, values = [], []
        for q_start, k_start in starts:
            q = q_ref[0, _rows(q_start, BLK, dil), :] * LOG2E
            qs = jnp.concatenate([jnp.where(lane_lo, q, 0.0), jnp.where(lane_lo, 0.0, q)], axis=0)
            kb = k_ref[0, _rows(k_start, n_keys, dil), :].astype(BF16)
            values.append(v_ref[0, _rows(k_start, n_keys, dil), :].astype(BF16))
            logits.append(lax.dot_general(qs.astype(BF16), kb, (((1,), (1,)), ((), ())),
                                          preferred_element_type=F32) + bias)
        stats = []
        for lg, vb in zip(logits, values):
            m = jnp.max(lg, axis=-1, keepdims=True)
            p = jnp.exp2(lg - m)
            s = jnp.sum(p, axis=-1, keepdims=True)
            pv = _dot(p.astype(BF16), vb)
            stats.append((jnp.where(lane_lo, m[:BLK], m[BLK:]),
                          jnp.where(lane_lo, s[:BLK], s[BLK:]),
                          jnp.where(lane_lo, pv[:BLK], pv[BLK:])))
        rows = [_rows(q_start, BLK, dil) for q_start, _ in starts]
        last = order == len(DILATIONS) - 1
        if order == 0:
            for r, (mm, ss, num) in zip(rows, stats):
                m_acc[r, :] = mm
                s_acc[r, :] = ss
                n_acc[r, :] = num
            return
        old = [(m_acc[r, :], s_acc[r, :], n_acc[r, :]) for r in rows]
        for r, (mm, ss, num), (m_old, s_old, n_old) in zip(rows, stats, old):
            m_new = jnp.maximum(m_old, mm)
            a = jnp.exp2(m_old - m_new)
            b = jnp.exp2(mm - m_new)
            s_new = a * s_old + b * ss
            n_new = a * n_old + b * num
            if last:
                o_ref[0, r, :] = n_new / s_new
            else:
                m_acc[r, :] = m_new
                s_acc[r, :] = s_new
                n_acc[r, :] = n_new

    for order, (br, dil) in enumerate(reversed(list(enumerate(DILATIONS)))):
        nb = seq // dil // BLK

        def starts_of(u, first, dil=dil):
            if first:
                return u, u
            lb = u // dil + 1
            r = u % dil
            return r + dil * BLK * lb, r + dil * BLK * (lb - 1)

        for first, n_units in ((True, dil), (False, dil * (nb - 1))):
            n_keys = BLK if first else 2 * BLK
            rem = n_units % ATTN_GROUP
            if rem:
                group(br, dil, [starts_of(u, first) for u in range(rem)], n_keys, order)

            def body(g, carry, br=br, dil=dil, order=order, first=first, rem=rem, n_keys=n_keys,
                     starts_of=starts_of):
                u0 = rem + g * ATTN_GROUP
                group(br, dil, [starts_of(u0 + j, first) for j in range(ATTN_GROUP)], n_keys, order)
                return carry

            lax.fori_loop(0, n_units // ATTN_GROUP, body, 0)


def _attn_prompt_call(q, k, v, bias2):
    bsz, seq, _ = q.shape
    blk = pl.BlockSpec((1, seq, 128), lambda b, hp: (b, 0, hp))
    return pl.pallas_call(
        functools.partial(_attn_prompt_kernel, seq=seq),
        grid=(bsz, N_HEAD_PAIRS),
        in_specs=[blk, blk, blk,
                  pl.BlockSpec((len(DILATIONS), 1, HEADS_PER_VREG * BLK, 2 * BLK),
                               lambda b, hp: (0, hp, 0, 0))],
        out_specs=blk,
        out_shape=jax.ShapeDtypeStruct((bsz, seq, D_A), F32),
        scratch_shapes=[pltpu.VMEM((seq, 128), F32)] * 3,
        compiler_params=pltpu.CompilerParams(dimension_semantics=("arbitrary", "arbitrary"),
                                             vmem_limit_bytes=VMEM_LIMIT),
        name="attn_prompt",
    )(q, k, v, bias2)


def _block_diag(blocks):
    n, r, c = blocks.shape
    eye = jnp.eye(n, dtype=blocks.dtype)
    return (eye[:, None, :, None] * blocks[:, :, None, :]).reshape(n * r, n * c)


def kernel(x_prompt, x_sample, cache_attn_k, cache_attn_v, cache_conv, cache_pool, rel_bias, g_ffn1, w_ffn1_gu, w_ffn1_down, g_mix, w_in, g_q, g_k, conv_w, conv_b, conv_ln_g, conv_ln_b, conv_pw, pool_w, pool_scale, w_out, g_ffn2, w_ffn2_gu, w_ffn2_down):
    bsz, seq, _ = x_prompt.shape
    nseq = x_sample.shape[0]
    depth = g_ffn1.shape[0]
    wbuf = cache_attn_k.shape[2]
    assert x_sample.shape[1] == 1 and wbuf == DILATIONS[-1] * SPAN and nseq == 128
    tm_prompt, tm_post, tm_sample = 512, 512, nseq

    head_of_lane = np.arange(MXU_WIDTH) // HEAD_DIM
    bd = jnp.asarray((head_of_lane[:, None] == head_of_lane[None, :]) / HEAD_DIM, BF16)
    bias_p = _bias_prompt_call(rel_bias).reshape(
        len(DILATIONS), N_HEAD_PAIRS, HEADS_PER_VREG * BLK, 2 * BLK)
    s_bucket, s_count = _sample_tables(wbuf)
    bias_s = _bias_sample_call(rel_bias, jnp.asarray(s_bucket))
    s_count = jnp.asarray(s_count)
    rb0 = rel_bias[0].reshape(N_HEADS, 1)
    ck = jnp.transpose(cache_attn_k, (0, 1, 3, 4, 2))
    cv = jnp.transpose(cache_attn_v, (0, 1, 3, 4, 2))
    cct = jnp.transpose(cache_conv, (0, 2, 1, 3))
    cpt = jnp.transpose(cache_pool, (0, 2, 1, 3))

    hp = x_prompt.reshape(bsz * seq, D_MODEL)
    hs = x_sample.reshape(nseq, D_MODEL)
    outs = {n: [] for n in ("pk", "pv", "pc", "pp", "sk", "sv", "sc", "sp")}
    wgu1, wdn1, win = (w.astype(BF16) for w in (w_ffn1_gu, w_ffn1_down, w_in))
    wgu2, wdn2, wout = (w.astype(BF16) for w in (w_ffn2_gu, w_ffn2_down, w_out))
    for l in range(depth):
        r1 = lambda a: a[l].reshape(1, -1)
        gq = jnp.tile(g_q[l], N_HEADS).reshape(1, D_A)
        gk = jnp.tile(g_k[l], N_HEADS).reshape(1, D_A)
        pw = conv_pw[l].astype(BF16)
        plw = _block_diag(pool_w[l]).astype(BF16)
        mixw = (conv_w[l], r1(conv_b), r1(conv_ln_g), r1(conv_ln_b), pw, plw, r1(pool_scale))
        pre_w = (r1(g_ffn1), wgu1, wdn1, r1(g_mix), win, gq, gk, bd)
        post_w = (wout, r1(g_ffn2), wgu2, wdn2)

        keep = min(wbuf, seq)
        hs, _, sk, sv, sglu, sc, sq_t, sk_t, sv_t = _pre_call(l, hs, *pre_w, tm=tm_sample)
        hp, q, k, v, glu, c, k_t, v_t = _pre_call(l, hp, *pre_w, tm=tm_prompt, seq=seq, keep=keep)

        sq = lambda a: a.reshape(bsz, seq, a.shape[-1])
        attn = _attn_prompt_call(sq(q), sq(k), sq(v), bias_p)
        sample = (sq_t, sk_t, sv_t, bias_s, s_count, rb0, jnp.zeros((D_A, nseq), F32), ck, cv)
        hp, sattn_t = _post_prompt_call(
            l, hp, attn.reshape(bsz * seq, D_A), glu, c, mixw, post_w,
            sample, (0, nseq), tm=tm_post, seq=seq)
        to_cache = lambda a: a.reshape(bsz, N_HEADS, HEAD_DIM, keep).transpose(0, 3, 1, 2)
        outs["pk"].append(to_cache(k_t))
        outs["pv"].append(to_cache(v_t))
        outs["pc"].append(sq(glu)[:, seq - (CONV_WIDTH - 1):])
        outs["pp"].append(sq(c)[:, seq - POOL_PREFIX:])

        hs = _post_sample_call(l, hs, sattn_t, sglu, sc, cct, cpt, mixw, post_w)
        outs["sk"].append(sk.reshape(nseq, 1, N_HEADS, HEAD_DIM))
        outs["sv"].append(sv.reshape(nseq, 1, N_HEADS, HEAD_DIM))
        outs["sc"].append(jnp.concatenate([cache_conv[l][:, 1:], sglu[:, None, :]], axis=1))
        outs["sp"].append(jnp.concatenate([cache_pool[l][:, 1:], sc[:, None, :]], axis=1))

    st = lambda n: jnp.stack(outs[n])
    return (hp.reshape(bsz, seq, D_MODEL), hs.reshape(nseq, 1, D_MODEL),
            st("pk"), st("pv"), st("pc"), st("pp"), st("sk"), st("sv"), st("sc"), st("sp"))
```

```python
import functools
import math

import numpy as np
import jax
import jax.numpy as jnp
from jax import lax
from jax.experimental import pallas as pl
from jax.experimental.pallas import tpu as pltpu

F32 = jnp.float32
BF16 = jnp.bfloat16

D_MODEL = 1024
HEAD_DIM = 64
N_HEADS = 8
D_A = N_HEADS * HEAD_DIM
D_B = 256
D_C = 256
D_IN = 3 * D_A + 2 * D_B + D_C
D_FF = 2816
DILATIONS = (1, 4, 16)
SPAN = 128
BLK = 128
CONV_WIDTH = 31
POOL_WINDOWS = (2, 4, 8, 16)
POOL_GROUP = 64
POOL_PREFIX = 15
NUM_BUCKETS = 32
MAX_EXACT = 16
REL_MAX_DIST = 2048
EPS = 1e-6
NEG_INF = -1e30

HEADS_PER_VREG = 128 // HEAD_DIM
N_HEAD_PAIRS = N_HEADS // HEADS_PER_VREG
MXU_WIDTH = 256
HALO = 32
FF_CHUNK = 256
ATTN_GROUP = 8
SA_UNITS_PER_SEQ = 1
SA_SLOTS = 2
LOG2E = math.log2(math.e)
VMEM_LIMIT = 60 * 1024 * 1024


def _bucket_np(dist):
    dist = np.asarray(dist, np.int64)
    ratio = np.log(np.maximum(dist, 1).astype(np.float32) / np.float32(MAX_EXACT))
    large = MAX_EXACT + (ratio / np.float32(math.log(REL_MAX_DIST / MAX_EXACT))
                         * np.float32(NUM_BUCKETS - MAX_EXACT)).astype(np.int32)
    return np.where(dist < MAX_EXACT, dist, np.minimum(large, NUM_BUCKETS - 1)).astype(np.int32)


def _prompt_bucket_table():
    qi = np.arange(BLK)[:, None]
    kk = np.arange(2 * BLK)[None, :]
    dist = BLK + qi - kk
    ok = (dist >= 0) & (dist <= SPAN)
    tabs = []
    for dil in DILATIONS:
        b = _bucket_np(dil * np.clip(dist, 0, SPAN))
        tabs.append(np.where(ok, b, -1))
    return np.stack(tabs).astype(np.int32)


def _sample_tables(wbuf):
    dist = wbuf - np.arange(wbuf)
    count = np.zeros(wbuf, np.int32)
    for dil in DILATIONS:
        count += ((dist % dil == 0) & (dist // dil <= SPAN)).astype(np.int32)
    bucket = np.where(count > 0, _bucket_np(dist), -1).astype(np.int32)
    return bucket[None, :], count.astype(np.float32)[None, :]


def _dot(a, b):
    return jnp.dot(a, b, preferred_element_type=F32)


def _rms(x, g):
    ms = jnp.mean(x * x, axis=-1, keepdims=True)
    return x * lax.rsqrt(ms + EPS) * g


def _silu(x):
    return x * jax.nn.sigmoid(x)


FFN_WEAVE_SLOTS = 3 * (D_FF // FF_CHUNK)


def _weave_plan(units):
    plan = {}
    for n, (wait, work) in enumerate(units):
        chunk = (n * (FFN_WEAVE_SLOTS // 3)) // len(units)
        assert 3 * chunk - 1 not in plan
        plan[3 * chunk - 1] = wait
        plan[3 * chunk + 2] = work
    return plan


def _swiglu_residual(h, g_ref, wgu_ref, wdn_ref, weave=None):
    weave = weave or {}
    xn = _rms(h, g_ref[...]).astype(BF16)
    if -1 in weave:
        weave[-1]()
    acc = None
    for c in range(D_FF // FF_CHUNK):
        lo = c * FF_CHUNK

        def after(n, c=c):
            if 3 * c + n in weave:
                weave[3 * c + n]()

        g = _dot(xn, wgu_ref[:, lo:lo + FF_CHUNK])
        after(0)
        u = _dot(xn, wgu_ref[:, D_FF + lo:D_FF + lo + FF_CHUNK])
        after(1)
        a = (_silu(g) * u).astype(BF16)
        d = _dot(a, wdn_ref[lo:lo + FF_CHUNK, :])
        acc = d if acc is None else acc + d
        after(2)
    return h + 0.5 * acc


def _head_rms(x, bd_ref, g):
    w = bd_ref.shape[0]
    x2 = (x * x).astype(BF16)
    ms = jnp.concatenate([_dot(x2[:, lo:lo + w], bd_ref[...]) for lo in range(0, D_A, w)], axis=1)
    return x * lax.rsqrt(ms + EPS) * g


def _layernorm(x, g, b):
    mu = jnp.mean(x, axis=-1, keepdims=True)
    xc = x - mu
    var = jnp.mean(xc * xc, axis=-1, keepdims=True)
    return xc * lax.rsqrt(var + EPS) * g + b


def _pre_kernel(h_ref, gf_ref, wgu_ref, wdn_ref, gm_ref, win_ref, gq_ref, gk_ref, bd_ref, *rest,
                n_t, q_scale, sa=None):
    weave = None
    if sa is not None:
        sa_in, rest, sa_scratch = rest[:N_SA_INPUTS], rest[N_SA_INPUTS:-3], rest[-3:]
        rest, sa_ref = rest[:-1], rest[-1]
        layer, n_steps, seq_lo, n_host = sa
        weave = _weave_plan(_sample_attn_units(pl.program_id(0), n_steps, layer, seq_lo, n_host,
                                               sa_in, sa_ref, sa_scratch))
    h1_ref, q_ref, k_ref, v_ref, glu_ref, c_ref = rest[:6]
    t_refs = rest[6:]
    assert len(t_refs) == n_t
    h1 = _swiglu_residual(h_ref[...], gf_ref, wgu_ref, wdn_ref, weave)
    h1_ref[...] = h1
    xn = _rms(h1, gm_ref[...]).astype(BF16)
    q = _head_rms(_dot(xn, win_ref[:, 0:D_A]), bd_ref, gq_ref[...]) * q_scale
    k = _head_rms(_dot(xn, win_ref[:, D_A:2 * D_A]), bd_ref, gk_ref[...])
    v = _dot(xn, win_ref[:, 2 * D_A:3 * D_A])
    q_ref[...] = q
    k_ref[...] = k
    v_ref[...] = v
    for t_ref, val in zip(t_refs, (q, k, v)[3 - len(t_refs):]):
        t_ref[...] = val.T.reshape(t_ref.shape)
    o = 3 * D_A
    b_val = _dot(xn, win_ref[:, o:o + D_B])
    b_gate = _dot(xn, win_ref[:, o + D_B:o + 2 * D_B])
    glu_ref[...] = b_val * jax.nn.sigmoid(b_gate)
    c_ref[...] = _dot(xn, win_ref[:, o + 2 * D_B:o + 2 * D_B + D_C])


def _const_spec(shape):
    nd = len(shape)
    return pl.BlockSpec(shape, lambda i, _nd=nd: (0,) * _nd, pipeline_mode=pl.Buffered(1))


def _layer_spec(shape, layer):
    nd = len(shape)
    return pl.BlockSpec((None,) + tuple(shape), lambda i, _nd=nd: (layer,) + (0,) * _nd,
                        pipeline_mode=pl.Buffered(1))


def _pre_call(layer, h, gf, wgu, wdn, gm, win, gq, gk, bd, tm, seq=None, keep=None, sample=None,
              sa_seqs=None):
    t = h.shape[0]
    n_steps = t // tm
    row = lambda w: pl.BlockSpec((tm, w), lambda i: (i, 0))
    in_specs = [row(D_MODEL), _const_spec((1, D_MODEL)), _layer_spec((D_MODEL, 2 * D_FF), layer),
                _layer_spec((D_FF, D_MODEL), layer), _const_spec((1, D_MODEL)),
                _layer_spec((D_MODEL, D_IN), layer), _const_spec((1, D_A)), _const_spec((1, D_A)),
                _const_spec((MXU_WIDTH, MXU_WIDTH))]
    out_specs = [row(D_MODEL), row(D_A), row(D_A), row(D_A), row(D_B), row(D_C)]
    out_shape = [jax.ShapeDtypeStruct((t, w), F32) for w in (D_MODEL, D_A, D_A, D_A, D_B, D_C)]
    if seq is not None:
        per_seq, skip = seq // tm, (seq - keep) // tm
        t_spec = pl.BlockSpec((1, D_A, tm),
                              lambda i: (i // per_seq, 0, jnp.maximum(i % per_seq - skip, 0)))
        out_specs += [t_spec] * 2
        out_shape += [jax.ShapeDtypeStruct((t // seq, D_A, keep), F32)] * 2
    else:
        out_specs += [pl.BlockSpec((D_A, tm), lambda i: (0, i))] * 3
        out_shape += [jax.ShapeDtypeStruct((D_A, t), F32)] * 3
    n_t = len(out_specs) - 6
    scratch, sa, operands = [], None, ()
    if sample is not None:
        sa_in, sa_out, sa_shape, scratch = _sample_attn_specs(sample)
        in_specs += sa_in
        out_specs.append(sa_out)
        out_shape.append(sa_shape)
        sa, operands = (layer, n_steps) + tuple(sa_seqs), tuple(sample)
    return pl.pallas_call(
        functools.partial(_pre_kernel, n_t=n_t, sa=sa,
                          q_scale=HEAD_DIM ** -0.5 * (LOG2E if seq is not None else 1.0)),
        grid=(n_steps,),
        in_specs=in_specs,
        out_specs=out_specs,
        out_shape=out_shape,
        scratch_shapes=scratch,
        compiler_params=pltpu.CompilerParams(dimension_semantics=("arbitrary",),
                                             vmem_limit_bytes=VMEM_LIMIT),
        name="pre_ffn_inproj",
    )(h, gf, wgu, wdn, gm, win, gq, gk, bd, *operands)


def _pool_lane_consts(shape):
    grp = lax.broadcasted_iota(jnp.int32, shape, len(shape) - 1) // POOL_GROUP
    win = jnp.full(shape, POOL_WINDOWS[0], jnp.int32)
    for g in range(1, len(POOL_WINDOWS)):
        win = jnp.where(grp == g, POOL_WINDOWS[g], win)
    return win


def _conv_tail(y, cb_ref, lng_ref, lnb_ref, pw_ref):
    y = _layernorm(y + cb_ref[...], lng_ref[...], lnb_ref[...])
    return _dot(_silu(y).astype(BF16), pw_ref[...])


def _pool_tail(tot, x, cnt, plw_ref, psc_ref):
    d = tot / cnt - x
    return _dot(d.astype(BF16), plw_ref[...]) * psc_ref[...]


def _mix_ffn(h, attn, conv, pool, wout_ref, gf_ref, wgu_ref, wdn_ref, weave=None):
    mix = _dot(attn.astype(BF16), wout_ref[0:D_A, :])
    mix = mix + _dot(conv.astype(BF16), wout_ref[D_A:D_A + D_B, :])
    mix = mix + _dot(pool.astype(BF16), wout_ref[D_A + D_B:D_A + D_B + D_C, :])
    return _swiglu_residual(h + mix, gf_ref, wgu_ref, wdn_ref, weave)


N_SA_INPUTS = 9


def _sample_attn_units(i, n_steps, layer, seq_lo, n_host, in_refs, o_ref, scratch):
    qt_ref, kt_ref, vt_ref, sbias_ref, cnt_ref, rb0_ref, init_ref, ck_hbm, cv_hbm = in_refs
    kbuf, vbuf, sems = scratch
    n_seq = qt_ref.shape[1]
    units = SA_UNITS_PER_SEQ * n_host // n_steps
    assert units * n_steps == SA_UNITS_PER_SEQ * n_host and units % SA_SLOTS == 0
    heads = N_HEADS // SA_UNITS_PER_SEQ

    def seq_of(step, j):
        return seq_lo + step * (units // SA_UNITS_PER_SEQ) + j // SA_UNITS_PER_SEQ

    def copies(step, j):
        g, slot = j % SA_UNITS_PER_SEQ, j % SA_SLOTS
        return [pltpu.make_async_copy(hbm.at[layer, seq_of(step, j), pl.ds(g * heads, heads)],
                                      buf.at[slot], sems.at[n, slot])
                for n, (hbm, buf) in enumerate(((ck_hbm, kbuf), (cv_hbm, vbuf)))]

    def start(step, j):
        for cp in copies(step, j):
            cp.start()

    @pl.when(i == 0)
    def _():
        o_ref[...] = init_ref[...]
        for j in range(SA_SLOTS):
            start(i, j)

    lane = lax.broadcasted_iota(jnp.int32, (D_A, n_seq), 1)
    head = lambda a, h: a[h * HEAD_DIM:(h + 1) * HEAD_DIM]
    cols = {}

    def wait(j):
        for cp in copies(i, j):
            cp.wait()

    def refill(j):
        nxt = j + SA_SLOTS
        if nxt < units:
            start(i, nxt)
        else:
            @pl.when(i + 1 < n_steps)
            def _():
                start(i + 1, nxt - units)

    def unit_pieces(j):
        g, slot = j % SA_UNITS_PER_SEQ, j % SA_SLOTS
        h0 = g * heads
        st = {"rows": [], "new": [], "outs": []}

        def keys(hh_lo, hh_hi):
            if hh_lo == 0:
                st["s"] = seq_of(i, j)
                if j // SA_UNITS_PER_SEQ not in cols:
                    sel = lane == st["s"]
                    cols[j // SA_UNITS_PER_SEQ] = [
                        jnp.sum(jnp.where(sel, r[...], 0.0), axis=1, keepdims=True)
                        for r in (qt_ref, kt_ref, vt_ref)]
            qc, kc, _ = cols[j // SA_UNITS_PER_SEQ]
            for hh in range(hh_lo, hh_hi):
                st["rows"].append(jnp.sum(kbuf[slot, hh] * head(qc, h0 + hh), axis=0, keepdims=True))
                st["new"].append(jnp.sum(head(qc, h0 + hh) * head(kc, h0 + hh), axis=0,
                                         keepdims=True))

        def softmax():
            lg = jnp.concatenate(st["rows"], axis=0) + sbias_ref[h0:h0 + heads, :]
            lg0 = jnp.concatenate(st["new"], axis=0) + rb0_ref[h0:h0 + heads, :]
            m = jnp.maximum(jnp.max(lg, axis=1, keepdims=True), lg0)
            st["p"] = jnp.exp(lg - m) * cnt_ref[...]
            st["p0"] = len(DILATIONS) * jnp.exp(lg0 - m)
            st["den"] = jnp.sum(st["p"], axis=1, keepdims=True) + st["p0"]

        def values(hh_lo, hh_hi):
            vc = cols[j // SA_UNITS_PER_SEQ][2]
            for hh in range(hh_lo, hh_hi):
                num = jnp.sum(vbuf[slot, hh] * st["p"][hh:hh + 1, :], axis=1, keepdims=True)
                num = num + st["p0"][hh:hh + 1] * head(vc, h0 + hh)
                st["outs"].append(num / st["den"][hh:hh + 1])
            if hh_hi == heads:
                r0, r1 = h0 * HEAD_DIM, (h0 + heads) * HEAD_DIM
                mine = lax.broadcasted_iota(jnp.int32, (r1 - r0, n_seq), 1) == st["s"]
                o_ref[r0:r1, :] = jnp.where(mine, jnp.concatenate(st["outs"], axis=0),
                                            o_ref[r0:r1, :])
                refill(j)

        def whole():
            keys(0, heads)
            softmax()
            values(0, heads)

        return [functools.partial(wait, j), whole]

    return [unit_pieces(j) for j in range(units)]


def _prompt_mixer_pieces(tile, g_ref, c_ref, gx, cx, conv_dst, pool_dst, mixer_w, tm):
    cw_ref, cb_ref, lng_ref, lnb_ref, pw_ref, plw_ref, psc_ref = mixer_w
    ext = tm + 8
    state = {}

    def load():
        keep = (tile > 0).astype(F32)
        gx[0:HALO, :] = gx[0:HALO, :] * keep
        gx[HALO:HALO + tm, :] = g_ref[...]
        cx[0:HALO, :] = cx[0:HALO, :] * keep
        cx[HALO:HALO + tm, :] = c_ref[...]

    def taps(b_lo, b_hi):
        y = state.get("y")
        for b in range(b_lo, b_hi):
            part = None
            for a in range(-(-CONV_WIDTH // 8)):
                lag = 8 * a + b
                if lag >= CONV_WIDTH:
                    continue
                lo = HALO - 8 - 8 * a
                term = cw_ref[CONV_WIDTH - 1 - lag:CONV_WIDTH - lag, :] * gx[lo:lo + ext, :]
                part = term if part is None else part + term
            if b:
                part = pltpu.roll(part, b, 0)
            y = part if y is None else y + part
        state["y"] = y

    def conv_tail():
        conv_dst[...] = _conv_tail(state["y"][8:], cb_ref, lng_ref, lnb_ref, pw_ref).astype(BF16)

    def pool():
        x = c_ref[...]
        win = _pool_lane_consts((tm, D_C))
        run = cx[...]
        tot = jnp.zeros_like(x)
        prev_w = 1
        for w in POOL_WINDOWS:
            assert w == 2 * prev_w
            run = run + pltpu.roll(run, prev_w, 0)
            tot = jnp.where(win == w, run[HALO:], tot)
            prev_w = w
        pos = tile * tm + lax.broadcasted_iota(jnp.int32, (tm, D_C), 0)
        cnt = jnp.minimum(win, pos + 1).astype(F32)
        pool_dst[...] = _pool_tail(tot, x, cnt, plw_ref, psc_ref).astype(BF16)

    def first():
        load()
        taps(0, 1)

    return [first] + [functools.partial(taps, b, b + 1) for b in range(1, 8)] + [conv_tail, pool]


def _post_prompt_kernel(h_ref, attn_ref, g_ref, gh_ref, c_ref, ch_ref,
                        cw_ref, cb_ref, lng_ref, lnb_ref, pw_ref, plw_ref, psc_ref,
                        wout_ref, gf_ref, wgu_ref, wdn_ref, *rest,
                        tm, per_seq, layer, n_steps, sa_seqs):
    sa_in, outs_scratch, sa_scratch = rest[:N_SA_INPUTS], rest[N_SA_INPUTS:-3], rest[-3:]
    out_ref, sa_ref, gx, cx, conv_t, pool_t = outs_scratch
    mixer_w = (cw_ref, cb_ref, lng_ref, lnb_ref, pw_ref, plw_ref, psc_ref)
    i = pl.program_id(0)
    units = _sample_attn_units(i, n_steps, layer, *sa_seqs, sa_in, sa_ref, sa_scratch)
    gx[0:HALO, :] = gh_ref[...]
    cx[0:HALO, :] = ch_ref[...]
    for piece in _prompt_mixer_pieces(i % per_seq, g_ref, c_ref, gx, cx, conv_t, pool_t, mixer_w, tm):
        piece()
    out_ref[...] = _mix_ffn(h_ref[...], attn_ref[...], conv_t[...], pool_t[...],
                            wout_ref, gf_ref, wgu_ref, wdn_ref, _weave_plan(units))


def _post_sample_kernel(h_ref, attn_ref, g_ref, c_ref, cc_ref, cp_ref,
                        cw_ref, cb_ref, lng_ref, lnb_ref, pw_ref, plw_ref, psc_ref,
                        wout_ref, gf_ref, wgu_ref, wdn_ref, out_ref):
    n_hist = CONV_WIDTH - 1
    y = g_ref[...] * cw_ref[n_hist:n_hist + 1, :]
    for j in range(n_hist):
        y = y + cc_ref[0, j] * cw_ref[j:j + 1, :]
    conv = _conv_tail(y, cb_ref, lng_ref, lnb_ref, pw_ref)

    x = c_ref[...]
    win = _pool_lane_consts(x.shape)
    run = x
    tot = jnp.zeros_like(x)
    prev_w = 1
    for w in POOL_WINDOWS:
        for sft in range(prev_w, w):
            run = run + cp_ref[0, POOL_PREFIX - sft]
        tot = jnp.where(win == w, run, tot)
        prev_w = w
    pool = _pool_tail(tot, x, win.astype(F32), plw_ref, psc_ref)

    out_ref[...] = _mix_ffn(h_ref[...], attn_ref[...].T, conv, pool, wout_ref, gf_ref, wgu_ref, wdn_ref)


def _mixer_weight_specs(layer):
    return [_const_spec((CONV_WIDTH, D_B)), _const_spec((1, D_B)), _const_spec((1, D_B)),
            _const_spec((1, D_B)), _const_spec((D_B, D_B)), _const_spec((D_C, D_C)),
            _const_spec((1, D_C)),
            _layer_spec((D_MODEL, D_MODEL), layer), _const_spec((1, D_MODEL)),
            _layer_spec((D_MODEL, 2 * D_FF), layer), _layer_spec((D_FF, D_MODEL), layer)]


def _sample_attn_specs(sample):
    assert len(sample) == N_SA_INPUTS
    n_seq, wbuf = sample[0].shape[1], sample[-1].shape[-1]
    heads = N_HEADS // SA_UNITS_PER_SEQ
    cst = lambda shape: pl.BlockSpec(shape, lambda i: (0,) * len(shape))
    hbm = pl.BlockSpec(memory_space=pl.ANY)
    in_specs = [cst((D_A, n_seq))] * 3 + [cst((N_HEADS, wbuf)), cst((1, wbuf)), cst((N_HEADS, 1)),
                                          cst((D_A, n_seq)), hbm, hbm]
    scratch = [pltpu.VMEM((SA_SLOTS, heads, HEAD_DIM, wbuf), F32),
               pltpu.VMEM((SA_SLOTS, heads, HEAD_DIM, wbuf), F32),
               pltpu.SemaphoreType.DMA((2, SA_SLOTS))]
    return in_specs, cst((D_A, n_seq)), jax.ShapeDtypeStruct((D_A, n_seq), F32), scratch


def _post_prompt_call(layer, h, attn, glu, c, mixw, postw, sample, sa_seqs, tm, seq):
    t = h.shape[0]
    n_steps = t // tm
    per = tm // HALO
    row = lambda w: pl.BlockSpec((tm, w), lambda i: (i, 0))
    halo = lambda w: pl.BlockSpec((HALO, w), lambda i: (jnp.maximum(i * per - 1, 0), 0))
    sa_in, sa_out, sa_shape, sa_scratch = _sample_attn_specs(sample)
    return pl.pallas_call(
        functools.partial(_post_prompt_kernel, tm=tm, per_seq=seq // tm, layer=layer,
                          n_steps=n_steps, sa_seqs=sa_seqs),
        grid=(n_steps,),
        in_specs=[row(D_MODEL), row(D_A), row(D_B), halo(D_B), row(D_C), halo(D_C)]
        + _mixer_weight_specs(layer) + sa_in,
        out_specs=[row(D_MODEL), sa_out],
        out_shape=[jax.ShapeDtypeStruct((t, D_MODEL), F32), sa_shape],
        scratch_shapes=[pltpu.VMEM((HALO + tm, D_B), F32), pltpu.VMEM((HALO + tm, D_C), F32),
                        pltpu.VMEM((tm, D_B), BF16), pltpu.VMEM((tm, D_C), BF16)] + sa_scratch,
        compiler_params=pltpu.CompilerParams(dimension_semantics=("arbitrary",),
                                             vmem_limit_bytes=VMEM_LIMIT),
        name="post_prompt",
    )(h, attn, glu, glu, c, c, *mixw, *postw, *sample)


def _post_sample_call(layer, h, attn, glu, c, cct, cpt, mixw, postw):
    t = h.shape[0]
    full = lambda w: pl.BlockSpec((t, w), lambda i: (0, 0))
    return pl.pallas_call(
        _post_sample_kernel,
        grid=(1,),
        in_specs=[full(D_MODEL), pl.BlockSpec((D_A, t), lambda i: (0, 0)), full(D_B), full(D_C),
                  pl.BlockSpec((1, CONV_WIDTH - 1, t, D_B), lambda i: (layer, 0, 0, 0)),
                  pl.BlockSpec((1, POOL_PREFIX, t, D_C), lambda i: (layer, 0, 0, 0))]
        + _mixer_weight_specs(layer),
        out_specs=full(D_MODEL),
        out_shape=jax.ShapeDtypeStruct((t, D_MODEL), F32),
        compiler_params=pltpu.CompilerParams(dimension_semantics=("arbitrary",),
                                             vmem_limit_bytes=VMEM_LIMIT),
        name="post_sample",
    )(h, attn, glu, c, cct, cpt, *mixw, *postw)


def _bias_prompt_kernel(rb_ref, tbl_ref, out_ref):
    tbl = tbl_ref[0]
    for h in range(N_HEADS):
        def body(b, acc, h=h):
            return jnp.where(tbl == b, rb_ref[b, h], acc)

        acc = lax.fori_loop(0, NUM_BUCKETS, body, jnp.zeros(tbl.shape, F32))
        out_ref[0, h] = jnp.where(tbl < 0, NEG_INF, acc * LOG2E)


def _bias_prompt_call(rel_bias):
    tbl = jnp.asarray(_prompt_bucket_table())
    nbr = len(DILATIONS)
    return pl.pallas_call(
        _bias_prompt_kernel,
        grid=(nbr,),
        in_specs=[pl.BlockSpec(memory_space=pltpu.SMEM),
                  pl.BlockSpec((1, BLK, 2 * BLK), lambda br: (br, 0, 0))],
        out_specs=pl.BlockSpec((1, N_HEADS, BLK, 2 * BLK), lambda br: (br, 0, 0, 0)),
        out_shape=jax.ShapeDtypeStruct((nbr, N_HEADS, BLK, 2 * BLK), F32),
        name="bias_prompt",
    )(rel_bias, tbl)


def _bias_sample_kernel(rb_ref, tbl_ref, out_ref):
    tbl = tbl_ref[...]
    for h in range(N_HEADS):
        def body(b, acc, h=h):
            return jnp.where(tbl == b, rb_ref[b, h], acc)

        acc = lax.fori_loop(0, NUM_BUCKETS, body, jnp.zeros(tbl.shape, F32))
        out_ref[h:h + 1, :] = jnp.where(tbl < 0, NEG_INF, acc)


def _bias_sample_call(rel_bias, bucket):
    return pl.pallas_call(
        _bias_sample_kernel,
        in_specs=[pl.BlockSpec(memory_space=pltpu.SMEM),
                  pl.BlockSpec(bucket.shape, lambda: (0, 0))],
        out_shape=jax.ShapeDtypeStruct((N_HEADS, bucket.shape[1]), F32),
        name="bias_sample",
    )(rel_bias, bucket)


def _rows(start, size, dil):
    return pl.ds(start, size) if dil == 1 else pl.ds(start, size, stride=dil)


def _attn_prompt_kernel(q_ref, k_ref, v_ref, bias_ref, o_ref, m_acc, s_acc, n_acc, *, seq):
    lane_lo = lax.broadcasted_iota(jnp.int32, (BLK, 128), 1) < HEAD_DIM

    def group(br, dil, starts, n_keys, order):
        bias = bias_ref[br, 0, :, 2 * BLK - n_keys:]
        logits, values = [], []
        for q_start, k_start in starts:
            q = q_ref[0, _rows(q_start, BLK, dil), :]
            qs = jnp.concatenate([jnp.where(lane_lo, q, 0.0), jnp.where(lane_lo, 0.0, q)], axis=0)
            kb = k_ref[0, _rows(k_start, n_keys, dil), :].astype(BF16)
            values.append(v_ref[0, _rows(k_start, n_keys, dil), :].astype(BF16))
            logits.append(lax.dot_general(qs.astype(BF16), kb, (((1,), (1,)), ((), ())),
                                          preferred_element_type=F32) + bias)
        stats = []
        ones = jnp.ones((n_keys, 128), BF16)
        for lg, vb in zip(logits, values):
            m = jnp.max(lg, axis=-1, keepdims=True)
            p = jnp.exp2(lg - m)
            pv = _dot(p.astype(BF16), jnp.concatenate([vb, ones], axis=1))
            num, s = pv[:, :128], pv[:, 128:]
            stats.append((jnp.where(lane_lo, m[:BLK], m[BLK:]),
                          jnp.where(lane_lo, s[:BLK], s[BLK:]),
                          jnp.where(lane_lo, num[:BLK], num[BLK:])))
        rows = [_rows(q_start, BLK, dil) for q_start, _ in starts]
        last = order == len(DILATIONS) - 1
        if order == 0:
            for r, (mm, ss, num) in zip(rows, stats):
                m_acc[r, :] = mm
                s_acc[r, :] = ss
                n_acc[r, :] = num
            return
        old = [(m_acc[r, :], s_acc[r, :], n_acc[r, :]) for r in rows]
        for r, (mm, ss, num), (m_old, s_old, n_old) in zip(rows, stats, old):
            m_new = jnp.maximum(m_old, mm)
            a = jnp.exp2(m_old - m_new)
            b = jnp.exp2(mm - m_new)
            s_new = a * s_old + b * ss
            n_new = a * n_old + b * num
            if last:
                o_ref[0, r, :] = n_new / s_new
            else:
                m_acc[r, :] = m_new
                s_acc[r, :] = s_new
                n_acc[r, :] = n_new

    for order, (br, dil) in enumerate(reversed(list(enumerate(DILATIONS)))):
        nb = seq // dil // BLK

        def starts_of(u, first, dil=dil):
            if first:
                return u, u
            lb = u // dil + 1
            r = u % dil
            return r + dil * BLK * lb, r + dil * BLK * (lb - 1)

        for first, n_units in ((True, dil), (False, dil * (nb - 1))):
            n_keys = BLK if first else 2 * BLK
            rem = n_units % ATTN_GROUP
            if rem:
                group(br, dil, [starts_of(u, first) for u in range(rem)], n_keys, order)

            def body(g, carry, br=br, dil=dil, order=order, first=first, rem=rem, n_keys=n_keys,
                     starts_of=starts_of):
                u0 = rem + g * ATTN_GROUP
                group(br, dil, [starts_of(u0 + j, first) for j in range(ATTN_GROUP)], n_keys, order)
                return carry

            lax.fori_loop(0, n_units // ATTN_GROUP, body, 0)


def _attn_prompt_call(q, k, v, bias2):
    bsz, seq, _ = q.shape
    blk = pl.BlockSpec((1, seq, 128), lambda b, hp: (b, 0, hp))
    return pl.pallas_call(
        functools.partial(_attn_prompt_kernel, seq=seq),
        grid=(bsz, N_HEAD_PAIRS),
        in_specs=[blk, blk, blk,
                  pl.BlockSpec((len(DILATIONS), 1, HEADS_PER_VREG * BLK, 2 * BLK),
                               lambda b, hp: (0, hp, 0, 0))],
        out_specs=blk,
        out_shape=jax.ShapeDtypeStruct((bsz, seq, D_A), F32),
        scratch_shapes=[pltpu.VMEM((seq, 128), F32)] * 3,
        compiler_params=pltpu.CompilerParams(dimension_semantics=("arbitrary", "arbitrary"),
                                             vmem_limit_bytes=VMEM_LIMIT),
        name="attn_prompt",
    )(q, k, v, bias2)


def _block_diag(blocks):
    n, r, c = blocks.shape
    eye = jnp.eye(n, dtype=blocks.dtype)
    return (eye[:, None, :, None] * blocks[:, :, None, :]).reshape(n * r, n * c)


def kernel(x_prompt, x_sample, cache_attn_k, cache_attn_v, cache_conv, cache_pool, rel_bias, g_ffn1, w_ffn1_gu, w_ffn1_down, g_mix, w_in, g_q, g_k, conv_w, conv_b, conv_ln_g, conv_ln_b, conv_pw, pool_w, pool_scale, w_out, g_ffn2, w_ffn2_gu, w_ffn2_down):
    bsz, seq, _ = x_prompt.shape
    nseq = x_sample.shape[0]
    depth = g_ffn1.shape[0]
    wbuf = cache_attn_k.shape[2]
    assert x_sample.shape[1] == 1 and wbuf == DILATIONS[-1] * SPAN and nseq == 128
    tm_prompt, tm_post, tm_sample = 512, 512, nseq

    head_of_lane = np.arange(MXU_WIDTH) // HEAD_DIM
    bd = jnp.asarray((head_of_lane[:, None] == head_of_lane[None, :]) / HEAD_DIM, BF16)
    bias_p = _bias_prompt_call(rel_bias).reshape(
        len(DILATIONS), N_HEAD_PAIRS, HEADS_PER_VREG * BLK, 2 * BLK)
    s_bucket, s_count = _sample_tables(wbuf)
    bias_s = _bias_sample_call(rel_bias, jnp.asarray(s_bucket))
    s_count = jnp.asarray(s_count)
    rb0 = rel_bias[0].reshape(N_HEADS, 1)
    ck = jnp.transpose(cache_attn_k, (0, 1, 3, 4, 2))
    cv = jnp.transpose(cache_attn_v, (0, 1, 3, 4, 2))
    cct = jnp.transpose(cache_conv, (0, 2, 1, 3))
    cpt = jnp.transpose(cache_pool, (0, 2, 1, 3))

    hp = x_prompt.reshape(bsz * seq, D_MODEL)
    hs = x_sample.reshape(nseq, D_MODEL)
    outs = {n: [] for n in ("pk", "pv", "pc", "pp", "sk", "sv", "sc", "sp")}
    wgu1, wdn1, win = (w.astype(BF16) for w in (w_ffn1_gu, w_ffn1_down, w_in))
    wgu2, wdn2, wout = (w.astype(BF16) for w in (w_ffn2_gu, w_ffn2_down, w_out))
    for l in range(depth):
        r1 = lambda a: a[l].reshape(1, -1)
        gq = jnp.tile(g_q[l], N_HEADS).reshape(1, D_A)
        gk = jnp.tile(g_k[l], N_HEADS).reshape(1, D_A)
        pw = conv_pw[l].astype(BF16)
        plw = _block_diag(pool_w[l]).astype(BF16)
        mixw = (conv_w[l], r1(conv_b), r1(conv_ln_g), r1(conv_ln_b), pw, plw, r1(pool_scale))
        pre_w = (r1(g_ffn1), wgu1, wdn1, r1(g_mix), win, gq, gk, bd)
        post_w = (wout, r1(g_ffn2), wgu2, wdn2)

        keep = min(wbuf, seq)
        hs, _, sk, sv, sglu, sc, sq_t, sk_t, sv_t = _pre_call(l, hs, *pre_w, tm=tm_sample)
        hp, q, k, v, glu, c, k_t, v_t = _pre_call(l, hp, *pre_w, tm=tm_prompt, seq=seq, keep=keep)

        sq = lambda a: a.reshape(bsz, seq, a.shape[-1])
        attn = _attn_prompt_call(sq(q), sq(k), sq(v), bias_p)
        sample = (sq_t, sk_t, sv_t, bias_s, s_count, rb0, jnp.zeros((D_A, nseq), F32), ck, cv)
        hp, sattn_t = _post_prompt_call(
            l, hp, attn.reshape(bsz * seq, D_A), glu, c, mixw, post_w,
            sample, (0, nseq), tm=tm_post, seq=seq)
        to_cache = lambda a: a.reshape(bsz, N_HEADS, HEAD_DIM, keep).transpose(0, 3, 1, 2)
        outs["pk"].append(to_cache(k_t))
        outs["pv"].append(to_cache(v_t))
        outs["pc"].append(sq(glu)[:, seq - (CONV_WIDTH - 1):])
        outs["pp"].append(sq(c)[:, seq - POOL_PREFIX:])

        hs = _post_sample_call(l, hs, sattn_t, sglu, sc, cct, cpt, mixw, post_w)
        outs["sk"].append(sk.reshape(nseq, 1, N_HEADS, HEAD_DIM))
        outs["sv"].append(sv.reshape(nseq, 1, N_HEADS, HEAD_DIM))
        outs["sc"].append(jnp.concatenate([cache_conv[l][:, 1:], sglu[:, None, :]], axis=1))
        outs["sp"].append(jnp.concatenate([cache_pool[l][:, 1:], sc[:, None, :]], axis=1))

    st = lambda n: jnp.stack(outs[n])
    return (hp.reshape(bsz, seq, D_MODEL), hs.reshape(nseq, 1, D_MODEL),
            st("pk"), st("pv"), st("pc"), st("pp"), st("sk"), st("sv"), st("sc"), st("sp"))
```

```python
import functools
import math

import numpy as np
import jax
import jax.numpy as jnp
from jax import lax
from jax.experimental import pallas as pl
from jax.experimental.pallas import tpu as pltpu

F32 = jnp.float32
BF16 = jnp.bfloat16

D_MODEL = 1024
HEAD_DIM = 64
N_HEADS = 8
D_A = N_HEADS * HEAD_DIM
D_B = 256
D_C = 256
D_IN = 3 * D_A + 2 * D_B + D_C
D_FF = 2816
DILATIONS = (1, 4, 16)
SPAN = 128
BLK = 128
CONV_WIDTH = 31
POOL_WINDOWS = (2, 4, 8, 16)
POOL_GROUP = 64
POOL_PREFIX = 15
NUM_BUCKETS = 32
MAX_EXACT = 16
REL_MAX_DIST = 2048
EPS = 1e-6
NEG_INF = -1e30

HEADS_PER_VREG = 128 // HEAD_DIM
N_HEAD_PAIRS = N_HEADS // HEADS_PER_VREG
MXU_WIDTH = 256
HALO = 32
FF_CHUNK = 256
ATTN_GROUP = 16
SA_UNITS_PER_SEQ = 1
SA_SLOTS = 2
LOG2E = math.log2(math.e)
VMEM_LIMIT = 60 * 1024 * 1024


def _bucket_np(dist):
    dist = np.asarray(dist, np.int64)
    ratio = np.log(np.maximum(dist, 1).astype(np.float32) / np.float32(MAX_EXACT))
    large = MAX_EXACT + (ratio / np.float32(math.log(REL_MAX_DIST / MAX_EXACT))
                         * np.float32(NUM_BUCKETS - MAX_EXACT)).astype(np.int32)
    return np.where(dist < MAX_EXACT, dist, np.minimum(large, NUM_BUCKETS - 1)).astype(np.int32)


def _prompt_bucket_table():
    qi = np.arange(BLK)[:, None]
    kk = np.arange(2 * BLK)[None, :]
    dist = BLK + qi - kk
    ok = (dist >= 0) & (dist <= SPAN)
    tabs = []
    for dil in DILATIONS:
        b = _bucket_np(dil * np.clip(dist, 0, SPAN))
        tabs.append(np.where(ok, b, -1))
    return np.stack(tabs).astype(np.int32)


def _sample_tables(wbuf):
    dist = wbuf - np.arange(wbuf)
    count = np.zeros(wbuf, np.int32)
    for dil in DILATIONS:
        count += ((dist % dil == 0) & (dist // dil <= SPAN)).astype(np.int32)
    bucket = np.where(count > 0, _bucket_np(dist), -1).astype(np.int32)
    return bucket[None, :], count.astype(np.float32)[None, :]


def _dot(a, b):
    return jnp.dot(a, b, preferred_element_type=F32)


def _rms(x, g):
    ms = jnp.mean(x * x, axis=-1, keepdims=True)
    return x * lax.rsqrt(ms + EPS) * g


def _silu(x):
    return x * jax.nn.sigmoid(x)


FFN_WEAVE_SLOTS = 3 * (D_FF // FF_CHUNK)


def _weave_plan(units):
    plan = {}
    for n, (wait, work) in enumerate(units):
        chunk = (n * (FFN_WEAVE_SLOTS // 3)) // len(units)
        assert 3 * chunk - 1 not in plan
        plan[3 * chunk - 1] = wait
        plan[3 * chunk + 2] = work
    return plan


def _swiglu_residual(h, g_ref, wgu_ref, wdn_ref, weave=None):
    weave = weave or {}
    xn = _rms(h, g_ref[...]).astype(BF16)
    if -1 in weave:
        weave[-1]()
    acc = None
    for c in range(D_FF // FF_CHUNK):
        lo = c * FF_CHUNK

        def after(n, c=c):
            if 3 * c + n in weave:
                weave[3 * c + n]()

        g = _dot(xn, wgu_ref[:, lo:lo + FF_CHUNK])
        after(0)
        u = _dot(xn, wgu_ref[:, D_FF + lo:D_FF + lo + FF_CHUNK])
        after(1)
        a = (_silu(g) * u).astype(BF16)
        d = _dot(a, wdn_ref[lo:lo + FF_CHUNK, :])
        acc = d if acc is None else acc + d
        after(2)
    return h + 0.5 * acc


def _head_rms(x, bd_ref, g):
    w = bd_ref.shape[0]
    x2 = (x * x).astype(BF16)
    ms = jnp.concatenate([_dot(x2[:, lo:lo + w], bd_ref[...]) for lo in range(0, D_A, w)], axis=1)
    return x * lax.rsqrt(ms + EPS) * g


def _layernorm(x, g, b):
    mu = jnp.mean(x, axis=-1, keepdims=True)
    xc = x - mu
    var = jnp.mean(xc * xc, axis=-1, keepdims=True)
    return xc * lax.rsqrt(var + EPS) * g + b


def _pre_kernel(h_ref, gf_ref, wgu_ref, wdn_ref, gm_ref, win_ref, gq_ref, gk_ref, bd_ref, *rest,
                n_t, q_scale, sa=None):
    weave = None
    if sa is not None:
        sa_in, rest, sa_scratch = rest[:N_SA_INPUTS], rest[N_SA_INPUTS:-3], rest[-3:]
        rest, sa_ref = rest[:-1], rest[-1]
        layer, n_steps, seq_lo, n_host = sa
        weave = _weave_plan(_sample_attn_units(pl.program_id(0), n_steps, layer, seq_lo, n_host,
                                               sa_in, sa_ref, sa_scratch))
    h1_ref, q_ref, k_ref, v_ref, glu_ref, c_ref = rest[:6]
    t_refs = rest[6:]
    assert len(t_refs) == n_t
    h1 = _swiglu_residual(h_ref[...], gf_ref, wgu_ref, wdn_ref, weave)
    h1_ref[...] = h1
    xn = _rms(h1, gm_ref[...]).astype(BF16)
    q = _head_rms(_dot(xn, win_ref[:, 0:D_A]), bd_ref, gq_ref[...]) * q_scale
    k = _head_rms(_dot(xn, win_ref[:, D_A:2 * D_A]), bd_ref, gk_ref[...])
    v = _dot(xn, win_ref[:, 2 * D_A:3 * D_A])
    q_ref[...] = q
    k_ref[...] = k
    v_ref[...] = v
    for t_ref, val in zip(t_refs, (q, k, v)[3 - len(t_refs):]):
        t_ref[...] = val.T.reshape(t_ref.shape)
    o = 3 * D_A
    b_val = _dot(xn, win_ref[:, o:o + D_B])
    b_gate = _dot(xn, win_ref[:, o + D_B:o + 2 * D_B])
    glu_ref[...] = b_val * jax.nn.sigmoid(b_gate)
    c_ref[...] = _dot(xn, win_ref[:, o + 2 * D_B:o + 2 * D_B + D_C])


def _const_spec(shape):
    nd = len(shape)
    return pl.BlockSpec(shape, lambda i, _nd=nd: (0,) * _nd, pipeline_mode=pl.Buffered(1))


def _layer_spec(shape, layer):
    nd = len(shape)
    return pl.BlockSpec((None,) + tuple(shape), lambda i, _nd=nd: (layer,) + (0,) * _nd,
                        pipeline_mode=pl.Buffered(1))


def _pre_call(layer, h, gf, wgu, wdn, gm, win, gq, gk, bd, tm, seq=None, keep=None, sample=None,
              sa_seqs=None):
    t = h.shape[0]
    n_steps = t // tm
    row = lambda w: pl.BlockSpec((tm, w), lambda i: (i, 0))
    in_specs = [row(D_MODEL), _const_spec((1, D_MODEL)), _layer_spec((D_MODEL, 2 * D_FF), layer),
                _layer_spec((D_FF, D_MODEL), layer), _const_spec((1, D_MODEL)),
                _layer_spec((D_MODEL, D_IN), layer), _const_spec((1, D_A)), _const_spec((1, D_A)),
                _const_spec((MXU_WIDTH, MXU_WIDTH))]
    out_specs = [row(D_MODEL), row(D_A), row(D_A), row(D_A), row(D_B), row(D_C)]
    out_shape = [jax.ShapeDtypeStruct((t, w), F32) for w in (D_MODEL, D_A, D_A, D_A, D_B, D_C)]
    if seq is not None:
        per_seq, skip = seq // tm, (seq - keep) // tm
        t_spec = pl.BlockSpec((1, D_A, tm),
                              lambda i: (i // per_seq, 0, jnp.maximum(i % per_seq - skip, 0)))
        out_specs += [t_spec] * 2
        out_shape += [jax.ShapeDtypeStruct((t // seq, D_A, keep), F32)] * 2
    else:
        out_specs += [pl.BlockSpec((D_A, tm), lambda i: (0, i))] * 3
        out_shape += [jax.ShapeDtypeStruct((D_A, t), F32)] * 3
    n_t = len(out_specs) - 6
    scratch, sa, operands = [], None, ()
    if sample is not None:
        sa_in, sa_out, sa_shape, scratch = _sample_attn_specs(sample)
        in_specs += sa_in
        out_specs.append(sa_out)
        out_shape.append(sa_shape)
        sa, operands = (layer, n_steps) + tuple(sa_seqs), tuple(sample)
    return pl.pallas_call(
        functools.partial(_pre_kernel, n_t=n_t, sa=sa,
                          q_scale=HEAD_DIM ** -0.5 * (LOG2E if seq is not None else 1.0)),
        grid=(n_steps,),
        in_specs=in_specs,
        out_specs=out_specs,
        out_shape=out_shape,
        scratch_shapes=scratch,
        compiler_params=pltpu.CompilerParams(dimension_semantics=("arbitrary",),
                                             vmem_limit_bytes=VMEM_LIMIT),
        name="pre_ffn_inproj",
    )(h, gf, wgu, wdn, gm, win, gq, gk, bd, *operands)


def _pool_lane_consts(shape):
    grp = lax.broadcasted_iota(jnp.int32, shape, len(shape) - 1) // POOL_GROUP
    win = jnp.full(shape, POOL_WINDOWS[0], jnp.int32)
    for g in range(1, len(POOL_WINDOWS)):
        win = jnp.where(grp == g, POOL_WINDOWS[g], win)
    return win


def _conv_tail(y, cb_ref, lng_ref, lnb_ref, pw_ref):
    y = _layernorm(y + cb_ref[...], lng_ref[...], lnb_ref[...])
    return _dot(_silu(y).astype(BF16), pw_ref[...])


def _pool_tail(tot, x, cnt, plw_ref, psc_ref):
    d = tot / cnt - x
    return _dot(d.astype(BF16), plw_ref[...]) * psc_ref[...]


def _mix_ffn(h, attn, conv, pool, wout_ref, gf_ref, wgu_ref, wdn_ref, weave=None):
    mix = _dot(attn.astype(BF16), wout_ref[0:D_A, :])
    mix = mix + _dot(conv.astype(BF16), wout_ref[D_A:D_A + D_B, :])
    mix = mix + _dot(pool.astype(BF16), wout_ref[D_A + D_B:D_A + D_B + D_C, :])
    return _swiglu_residual(h + mix, gf_ref, wgu_ref, wdn_ref, weave)


N_SA_INPUTS = 9


def _sample_attn_units(i, n_steps, layer, seq_lo, n_host, in_refs, o_ref, scratch):
    qt_ref, kt_ref, vt_ref, sbias_ref, cnt_ref, rb0_ref, init_ref, ck_hbm, cv_hbm = in_refs
    kbuf, vbuf, sems = scratch
    n_seq = qt_ref.shape[1]
    units = SA_UNITS_PER_SEQ * n_host // n_steps
    assert units * n_steps == SA_UNITS_PER_SEQ * n_host and units % SA_SLOTS == 0
    heads = N_HEADS // SA_UNITS_PER_SEQ

    def seq_of(step, j):
        return seq_lo + step * (units // SA_UNITS_PER_SEQ) + j // SA_UNITS_PER_SEQ

    def copies(step, j):
        g, slot = j % SA_UNITS_PER_SEQ, j % SA_SLOTS
        return [pltpu.make_async_copy(hbm.at[layer, seq_of(step, j), pl.ds(g * heads, heads)],
                                      buf.at[slot], sems.at[n, slot])
                for n, (hbm, buf) in enumerate(((ck_hbm, kbuf), (cv_hbm, vbuf)))]

    def start(step, j):
        for cp in copies(step, j):
            cp.start()

    @pl.when(i == 0)
    def _():
        o_ref[...] = init_ref[...]
        for j in range(SA_SLOTS):
            start(i, j)

    lane = lax.broadcasted_iota(jnp.int32, (D_A, n_seq), 1)
    head = lambda a, h: a[h * HEAD_DIM:(h + 1) * HEAD_DIM]
    cols = {}

    def wait(j):
        for cp in copies(i, j):
            cp.wait()

    def refill(j):
        nxt = j + SA_SLOTS
        if nxt < units:
            start(i, nxt)
        else:
            @pl.when(i + 1 < n_steps)
            def _():
                start(i + 1, nxt - units)

    def unit_pieces(j):
        g, slot = j % SA_UNITS_PER_SEQ, j % SA_SLOTS
        h0 = g * heads
        st = {"rows": [], "new": [], "outs": []}

        def keys(hh_lo, hh_hi):
            if hh_lo == 0:
                st["s"] = seq_of(i, j)
                if j // SA_UNITS_PER_SEQ not in cols:
                    sel = lane == st["s"]
                    cols[j // SA_UNITS_PER_SEQ] = [
                        jnp.sum(jnp.where(sel, r[...], 0.0), axis=1, keepdims=True)
                        for r in (qt_ref, kt_ref, vt_ref)]
            qc, kc, _ = cols[j // SA_UNITS_PER_SEQ]
            for hh in range(hh_lo, hh_hi):
                st["rows"].append(jnp.sum(kbuf[slot, hh] * head(qc, h0 + hh), axis=0, keepdims=True))
                st["new"].append(jnp.sum(head(qc, h0 + hh) * head(kc, h0 + hh), axis=0,
                                         keepdims=True))

        def softmax():
            lg = jnp.concatenate(st["rows"], axis=0) + sbias_ref[h0:h0 + heads, :]
            lg0 = jnp.concatenate(st["new"], axis=0) + rb0_ref[h0:h0 + heads, :]
            m = jnp.maximum(jnp.max(lg, axis=1, keepdims=True), lg0)
            st["p"] = jnp.exp(lg - m) * cnt_ref[...]
            st["p0"] = len(DILATIONS) * jnp.exp(lg0 - m)
            st["den"] = jnp.sum(st["p"], axis=1, keepdims=True) + st["p0"]

        def values(hh_lo, hh_hi):
            vc = cols[j // SA_UNITS_PER_SEQ][2]
            for hh in range(hh_lo, hh_hi):
                num = jnp.sum(vbuf[slot, hh] * st["p"][hh:hh + 1, :], axis=1, keepdims=True)
                num = num + st["p0"][hh:hh + 1] * head(vc, h0 + hh)
                st["outs"].append(num / st["den"][hh:hh + 1])
            if hh_hi == heads:
                r0, r1 = h0 * HEAD_DIM, (h0 + heads) * HEAD_DIM
                mine = lax.broadcasted_iota(jnp.int32, (r1 - r0, n_seq), 1) == st["s"]
                o_ref[r0:r1, :] = jnp.where(mine, jnp.concatenate(st["outs"], axis=0),
                                            o_ref[r0:r1, :])
                refill(j)

        def whole():
            keys(0, heads)
            softmax()
            values(0, heads)

        return [functools.partial(wait, j), whole]

    return [unit_pieces(j) for j in range(units)]


def _prompt_mixer_pieces(tile, g_ref, c_ref, gx, cx, conv_dst, pool_dst, mixer_w, tm):
    cw_ref, cb_ref, lng_ref, lnb_ref, pw_ref, plw_ref, psc_ref = mixer_w
    ext = tm + 8
    state = {}

    def load():
        keep = (tile > 0).astype(F32)
        gx[0:HALO, :] = gx[0:HALO, :] * keep
        gx[HALO:HALO + tm, :] = g_ref[...]
        cx[0:HALO, :] = cx[0:HALO, :] * keep
        cx[HALO:HALO + tm, :] = c_ref[...]

    def taps(b_lo, b_hi):
        y = state.get("y")
        for b in range(b_lo, b_hi):
            part = None
            for a in range(-(-CONV_WIDTH // 8)):
                lag = 8 * a + b
                if lag >= CONV_WIDTH:
                    continue
                lo = HALO - 8 - 8 * a
                term = cw_ref[CONV_WIDTH - 1 - lag:CONV_WIDTH - lag, :] * gx[lo:lo + ext, :]
                part = term if part is None else part + term
            if b:
                part = pltpu.roll(part, b, 0)
            y = part if y is None else y + part
        state["y"] = y

    def conv_tail():
        conv_dst[...] = _conv_tail(state["y"][8:], cb_ref, lng_ref, lnb_ref, pw_ref).astype(BF16)

    def pool():
        x = c_ref[...]
        win = _pool_lane_consts((tm, D_C))
        run = cx[...]
        tot = jnp.zeros_like(x)
        prev_w = 1
        for w in POOL_WINDOWS:
            assert w == 2 * prev_w
            run = run + pltpu.roll(run, prev_w, 0)
            tot = jnp.where(win == w, run[HALO:], tot)
            prev_w = w
        pos = tile * tm + lax.broadcasted_iota(jnp.int32, (tm, D_C), 0)
        cnt = jnp.minimum(win, pos + 1).astype(F32)
        pool_dst[...] = _pool_tail(tot, x, cnt, plw_ref, psc_ref).astype(BF16)

    def first():
        load()
        taps(0, 1)

    return [first] + [functools.partial(taps, b, b + 1) for b in range(1, 8)] + [conv_tail, pool]


def _post_prompt_kernel(h_ref, attn_ref, g_ref, gh_ref, c_ref, ch_ref,
                        cw_ref, cb_ref, lng_ref, lnb_ref, pw_ref, plw_ref, psc_ref,
                        wout_ref, gf_ref, wgu_ref, wdn_ref, *rest,
                        tm, per_seq, layer, n_steps, sa_seqs):
    sa_in, outs_scratch, sa_scratch = rest[:N_SA_INPUTS], rest[N_SA_INPUTS:-3], rest[-3:]
    out_ref, sa_ref, gx, cx, conv_t, pool_t = outs_scratch
    mixer_w = (cw_ref, cb_ref, lng_ref, lnb_ref, pw_ref, plw_ref, psc_ref)
    i = pl.program_id(0)
    units = _sample_attn_units(i, n_steps, layer, *sa_seqs, sa_in, sa_ref, sa_scratch)
    gx[0:HALO, :] = gh_ref[...]
    cx[0:HALO, :] = ch_ref[...]
    for piece in _prompt_mixer_pieces(i % per_seq, g_ref, c_ref, gx, cx, conv_t, pool_t, mixer_w, tm):
        piece()
    out_ref[...] = _mix_ffn(h_ref[...], attn_ref[...], conv_t[...], pool_t[...],
                            wout_ref, gf_ref, wgu_ref, wdn_ref, _weave_plan(units))


def _post_sample_kernel(h_ref, attn_ref, g_ref, c_ref, cc_ref, cp_ref,
                        cw_ref, cb_ref, lng_ref, lnb_ref, pw_ref, plw_ref, psc_ref,
                        wout_ref, gf_ref, wgu_ref, wdn_ref, out_ref):
    n_hist = CONV_WIDTH - 1
    y = g_ref[...] * cw_ref[n_hist:n_hist + 1, :]
    for j in range(n_hist):
        y = y + cc_ref[0, j] * cw_ref[j:j + 1, :]
    conv = _conv_tail(y, cb_ref, lng_ref, lnb_ref, pw_ref)

    x = c_ref[...]
    win = _pool_lane_consts(x.shape)
    run = x
    tot = jnp.zeros_like(x)
    prev_w = 1
    for w in POOL_WINDOWS:
        for sft in range(prev_w, w):
            run = run + cp_ref[0, POOL_PREFIX - sft]
        tot = jnp.where(win == w, run, tot)
        prev_w = w
    pool = _pool_tail(tot, x, win.astype(F32), plw_ref, psc_ref)

    out_ref[...] = _mix_ffn(h_ref[...], attn_ref[...].T, conv, pool, wout_ref, gf_ref, wgu_ref, wdn_ref)


def _mixer_weight_specs(layer):
    return [_const_spec((CONV_WIDTH, D_B)), _const_spec((1, D_B)), _const_spec((1, D_B)),
            _const_spec((1, D_B)), _const_spec((D_B, D_B)), _const_spec((D_C, D_C)),
            _const_spec((1, D_C)),
            _layer_spec((D_MODEL, D_MODEL), layer), _const_spec((1, D_MODEL)),
            _layer_spec((D_MODEL, 2 * D_FF), layer), _layer_spec((D_FF, D_MODEL), layer)]


def _sample_attn_specs(sample):
    assert len(sample) == N_SA_INPUTS
    n_seq, wbuf = sample[0].shape[1], sample[-1].shape[-1]
    heads = N_HEADS // SA_UNITS_PER_SEQ
    cst = lambda shape: pl.BlockSpec(shape, lambda i: (0,) * len(shape))
    hbm = pl.BlockSpec(memory_space=pl.ANY)
    in_specs = [cst((D_A, n_seq))] * 3 + [cst((N_HEADS, wbuf)), cst((1, wbuf)), cst((N_HEADS, 1)),
                                          cst((D_A, n_seq)), hbm, hbm]
    scratch = [pltpu.VMEM((SA_SLOTS, heads, HEAD_DIM, wbuf), F32),
               pltpu.VMEM((SA_SLOTS, heads, HEAD_DIM, wbuf), F32),
               pltpu.SemaphoreType.DMA((2, SA_SLOTS))]
    return in_specs, cst((D_A, n_seq)), jax.ShapeDtypeStruct((D_A, n_seq), F32), scratch


def _post_prompt_call(layer, h, attn, glu, c, mixw, postw, sample, sa_seqs, tm, seq):
    t = h.shape[0]
    n_steps = t // tm
    per = tm // HALO
    row = lambda w: pl.BlockSpec((tm, w), lambda i: (i, 0))
    halo = lambda w: pl.BlockSpec((HALO, w), lambda i: (jnp.maximum(i * per - 1, 0), 0))
    sa_in, sa_out, sa_shape, sa_scratch = _sample_attn_specs(sample)
    return pl.pallas_call(
        functools.partial(_post_prompt_kernel, tm=tm, per_seq=seq // tm, layer=layer,
                          n_steps=n_steps, sa_seqs=sa_seqs),
        grid=(n_steps,),
        in_specs=[row(D_MODEL), row(D_A), row(D_B), halo(D_B), row(D_C), halo(D_C)]
        + _mixer_weight_specs(layer) + sa_in,
        out_specs=[row(D_MODEL), sa_out],
        out_shape=[jax.ShapeDtypeStruct((t, D_MODEL), F32), sa_shape],
        scratch_shapes=[pltpu.VMEM((HALO + tm, D_B), F32), pltpu.VMEM((HALO + tm, D_C), F32),
                        pltpu.VMEM((tm, D_B), BF16), pltpu.VMEM((tm, D_C), BF16)] + sa_scratch,
        compiler_params=pltpu.CompilerParams(dimension_semantics=("arbitrary",),
                                             vmem_limit_bytes=VMEM_LIMIT),
        name="post_prompt",
    )(h, attn, glu, glu, c, c, *mixw, *postw, *sample)


def _post_sample_call(layer, h, attn, glu, c, cct, cpt, mixw, postw):
    t = h.shape[0]
    full = lambda w: pl.BlockSpec((t, w), lambda i: (0, 0))
    return pl.pallas_call(
        _post_sample_kernel,
        grid=(1,),
        in_specs=[full(D_MODEL), pl.BlockSpec((D_A, t), lambda i: (0, 0)), full(D_B), full(D_C),
                  pl.BlockSpec((1, CONV_WIDTH - 1, t, D_B), lambda i: (layer, 0, 0, 0)),
                  pl.BlockSpec((1, POOL_PREFIX, t, D_C), lambda i: (layer, 0, 0, 0))]
        + _mixer_weight_specs(layer),
        out_specs=full(D_MODEL),
        out_shape=jax.ShapeDtypeStruct((t, D_MODEL), F32),
        compiler_params=pltpu.CompilerParams(dimension_semantics=("arbitrary",),
                                             vmem_limit_bytes=VMEM_LIMIT),
        name="post_sample",
    )(h, attn, glu, c, cct, cpt, *mixw, *postw)


def _bias_prompt_kernel(rb_ref, tbl_ref, out_ref):
    tbl = tbl_ref[0]
    for h in range(N_HEADS):
        def body(b, acc, h=h):
            return jnp.where(tbl == b, rb_ref[b, h], acc)

        acc = lax.fori_loop(0, NUM_BUCKETS, body, jnp.zeros(tbl.shape, F32))
        out_ref[0, h] = jnp.where(tbl < 0, NEG_INF, acc * LOG2E)


def _bias_prompt_call(rel_bias):
    tbl = jnp.asarray(_prompt_bucket_table())
    nbr = len(DILATIONS)
    return pl.pallas_call(
        _bias_prompt_kernel,
        grid=(nbr,),
        in_specs=[pl.BlockSpec(memory_space=pltpu.SMEM),
                  pl.BlockSpec((1, BLK, 2 * BLK), lambda br: (br, 0, 0))],
        out_specs=pl.BlockSpec((1, N_HEADS, BLK, 2 * BLK), lambda br: (br, 0, 0, 0)),
        out_shape=jax.ShapeDtypeStruct((nbr, N_HEADS, BLK, 2 * BLK), F32),
        name="bias_prompt",
    )(rel_bias, tbl)


def _bias_sample_kernel(rb_ref, tbl_ref, out_ref):
    tbl = tbl_ref[...]
    for h in range(N_HEADS):
        def body(b, acc, h=h):
            return jnp.where(tbl == b, rb_ref[b, h], acc)

        acc = lax.fori_loop(0, NUM_BUCKETS, body, jnp.zeros(tbl.shape, F32))
        out_ref[h:h + 1, :] = jnp.where(tbl < 0, NEG_INF, acc)


def _bias_sample_call(rel_bias, bucket):
    return pl.pallas_call(
        _bias_sample_kernel,
        in_specs=[pl.BlockSpec(memory_space=pltpu.SMEM),
                  pl.BlockSpec(bucket.shape, lambda: (0, 0))],
        out_shape=jax.ShapeDtypeStruct((N_HEADS, bucket.shape[1]), F32),
        name="bias_sample",
    )(rel_bias, bucket)


def _rows(start, size, dil):
    return pl.ds(start, size) if dil == 1 else pl.ds(start, size, stride=dil)


def _attn_prompt_kernel(q_ref, k_ref, v_ref, bias_ref, o_ref, m_acc, s_acc, n_acc, *, seq):
    lane_lo = lax.broadcasted_iota(jnp.int32, (BLK, 128), 1) < HEAD_DIM

    def group(br, dil, starts, n_keys, order):
        bias = bias_ref[br, 0, :, 2 * BLK - n_keys:]
        logits, values = [], []
        for q_start, k_start in starts:
            q = q_ref[0, _rows(q_start, BLK, dil), :]
            qs = jnp.concatenate([jnp.where(lane_lo, q, 0.0), jnp.where(lane_lo, 0.0, q)], axis=0)
            kb = k_ref[0, _rows(k_start, n_keys, dil), :].astype(BF16)
            values.append(v_ref[0, _rows(k_start, n_keys, dil), :].astype(BF16))
            logits.append(lax.dot_general(qs.astype(BF16), kb, (((1,), (1,)), ((), ())),
                                          preferred_element_type=F32) + bias)
        stats = []
        ones = jnp.ones((n_keys, 128), BF16)
        for lg, vb in zip(logits, values):
            m = jnp.max(lg, axis=-1, keepdims=True)
            p = jnp.exp2(lg - m)
            pv = _dot(p.astype(BF16), jnp.concatenate([vb, ones], axis=1))
            num, s = pv[:, :128], pv[:, 128:]
            stats.append((jnp.where(lane_lo, m[:BLK], m[BLK:]),
                          jnp.where(lane_lo, s[:BLK], s[BLK:]),
                          jnp.where(lane_lo, num[:BLK], num[BLK:])))
        rows = [_rows(q_start, BLK, dil) for q_start, _ in starts]
        last = order == len(DILATIONS) - 1
        if order == 0:
            for r, (mm, ss, num) in zip(rows, stats):
                m_acc[r, :] = mm
                s_acc[r, :] = ss
                n_acc[r, :] = num
            return
        old = [(m_acc[r, :], s_acc[r, :], n_acc[r, :]) for r in rows]
        for r, (mm, ss, num), (m_old, s_old, n_old) in zip(rows, stats, old):
            m_new = jnp.maximum(m_old, mm)
            a = jnp.exp2(m_old - m_new)
            b = jnp.exp2(mm - m_new)
            s_new = a * s_old + b * ss
            n_new = a * n_old + b * num
            if last:
                o_ref[0, r, :] = n_new / s_new
            else:
                m_acc[r, :] = m_new
                s_acc[r, :] = s_new
                n_acc[r, :] = n_new

    for order, (br, dil) in enumerate(reversed(list(enumerate(DILATIONS)))):
        nb = seq // dil // BLK

        def starts_of(u, first, dil=dil):
            if first:
                return u, u
            lb = u // dil + 1
            r = u % dil
            return r + dil * BLK * lb, r + dil * BLK * (lb - 1)

        for first, n_units in ((True, dil), (False, dil * (nb - 1))):
            n_keys = BLK if first else 2 * BLK
            rem = n_units % ATTN_GROUP
            if rem:
                group(br, dil, [starts_of(u, first) for u in range(rem)], n_keys, order)

            def body(g, carry, br=br, dil=dil, order=order, first=first, rem=rem, n_keys=n_keys,
                     starts_of=starts_of):
                u0 = rem + g * ATTN_GROUP
                group(br, dil, [starts_of(u0 + j, first) for j in range(ATTN_GROUP)], n_keys, order)
                return carry

            lax.fori_loop(0, n_units // ATTN_GROUP, body, 0)


def _attn_prompt_call(q, k, v, bias2):
    bsz, seq, _ = q.shape
    blk = pl.BlockSpec((1, seq, 128), lambda b, hp: (b, 0, hp))
    return pl.pallas_call(
        functools.partial(_attn_prompt_kernel, seq=seq),
        grid=(bsz, N_HEAD_PAIRS),
        in_specs=[blk, blk, blk,
                  pl.BlockSpec((len(DILATIONS), 1, HEADS_PER_VREG * BLK, 2 * BLK),
                               lambda b, hp: (0, hp, 0, 0))],
        out_specs=blk,
        out_shape=jax.ShapeDtypeStruct((bsz, seq, D_A), F32),
        scratch_shapes=[pltpu.VMEM((seq, 128), F32)] * 3,
        compiler_params=pltpu.CompilerParams(dimension_semantics=("arbitrary", "arbitrary"),
                                             vmem_limit_bytes=VMEM_LIMIT),
        name="attn_prompt",
    )(q, k, v, bias2)


def _block_diag(blocks):
    n, r, c = blocks.shape
    eye = jnp.eye(n, dtype=blocks.dtype)
    return (eye[:, None, :, None] * blocks[:, :, None, :]).reshape(n * r, n * c)


def kernel(x_prompt, x_sample, cache_attn_k, cache_attn_v, cache_conv, cache_pool, rel_bias, g_ffn1, w_ffn1_gu, w_ffn1_down, g_mix, w_in, g_q, g_k, conv_w, conv_b, conv_ln_g, conv_ln_b, conv_pw, pool_w, pool_scale, w_out, g_ffn2, w_ffn2_gu, w_ffn2_down):
    bsz, seq, _ = x_prompt.shape
    nseq = x_sample.shape[0]
    depth = g_ffn1.shape[0]
    wbuf = cache_attn_k.shape[2]
    assert x_sample.shape[1] == 1 and wbuf == DILATIONS[-1] * SPAN and nseq == 128
    tm_prompt, tm_post, tm_sample = 512, 512, nseq

    head_of_lane = np.arange(MXU_WIDTH) // HEAD_DIM
    bd = jnp.asarray((head_of_lane[:, None] == head_of_lane[None, :]) / HEAD_DIM, BF16)
    bias_p = _bias_prompt_call(rel_bias).reshape(
        len(DILATIONS), N_HEAD_PAIRS, HEADS_PER_VREG * BLK, 2 * BLK)
    s_bucket, s_count = _sample_tables(wbuf)
    bias_s = _bias_sample_call(rel_bias, jnp.asarray(s_bucket))
    s_count = jnp.asarray(s_count)
    rb0 = rel_bias[0].reshape(N_HEADS, 1)
    ck = jnp.transpose(cache_attn_k, (0, 1, 3, 4, 2))
    cv = jnp.transpose(cache_attn_v, (0, 1, 3, 4, 2))
    cct = jnp.transpose(cache_conv, (0, 2, 1, 3))
    cpt = jnp.transpose(cache_pool, (0, 2, 1, 3))

    hp = x_prompt.reshape(bsz * seq, D_MODEL)
    hs = x_sample.reshape(nseq, D_MODEL)
    outs = {n: [] for n in ("pk", "pv", "pc", "pp", "sk", "sv", "sc", "sp")}
    wgu1, wdn1, win = (w.astype(BF16) for w in (w_ffn1_gu, w_ffn1_down, w_in))
    wgu2, wdn2, wout = (w.astype(BF16) for w in (w_ffn2_gu, w_ffn2_down, w_out))
    for l in range(depth):
        r1 = lambda a: a[l].reshape(1, -1)
        gq = jnp.tile(g_q[l], N_HEADS).reshape(1, D_A)
        gk = jnp.tile(g_k[l], N_HEADS).reshape(1, D_A)
        pw = conv_pw[l].astype(BF16)
        plw = _block_diag(pool_w[l]).astype(BF16)
        mixw = (conv_w[l], r1(conv_b), r1(conv_ln_g), r1(conv_ln_b), pw, plw, r1(pool_scale))
        pre_w = (r1(g_ffn1), wgu1, wdn1, r1(g_mix), win, gq, gk, bd)
        post_w = (wout, r1(g_ffn2), wgu2, wdn2)

        keep = min(wbuf, seq)
        hs, _, sk, sv, sglu, sc, sq_t, sk_t, sv_t = _pre_call(l, hs, *pre_w, tm=tm_sample)
        hp, q, k, v, glu, c, k_t, v_t = _pre_call(l, hp, *pre_w, tm=tm_prompt, seq=seq, keep=keep)

        sq = lambda a: a.reshape(bsz, seq, a.shape[-1])
        attn = _attn_prompt_call(sq(q), sq(k), sq(v), bias_p)
        sample = (sq_t, sk_t, sv_t, bias_s, s_count, rb0, jnp.zeros((D_A, nseq), F32), ck, cv)
        hp, sattn_t = _post_prompt_call(
            l, hp, attn.reshape(bsz * seq, D_A), glu, c, mixw, post_w,
            sample, (0, nseq), tm=tm_post, seq=seq)
        to_cache = lambda a: a.reshape(bsz, N_HEADS, HEAD_DIM, keep).transpose(0, 3, 1, 2)
        outs["pk"].append(to_cache(k_t))
        outs["pv"].append(to_cache(v_t))
        outs["pc"].append(sq(glu)[:, seq - (CONV_WIDTH - 1):])
        outs["pp"].append(sq(c)[:, seq - POOL_PREFIX:])

        hs = _post_sample_call(l, hs, sattn_t, sglu, sc, cct, cpt, mixw, post_w)
        outs["sk"].append(sk.reshape(nseq, 1, N_HEADS, HEAD_DIM))
        outs["sv"].append(sv.reshape(nseq, 1, N_HEADS, HEAD_DIM))
        outs["sc"].append(jnp.concatenate([cache_conv[l][:, 1:], sglu[:, None, :]], axis=1))
        outs["sp"].append(jnp.concatenate([cache_pool[l][:, 1:], sc[:, None, :]], axis=1))

    st = lambda n: jnp.stack(outs[n])
    return (hp.reshape(bsz, seq, D_MODEL), hs.reshape(nseq, 1, D_MODEL),
            st("pk"), st("pv"), st("pc"), st("pp"), st("sk"), st("sv"), st("sc"), st("sp"))
```

```python
import functools
import math

import numpy as np
import jax
import jax.numpy as jnp
from jax import lax
from jax.experimental import pallas as pl
from jax.experimental.pallas import tpu as pltpu

F32 = jnp.float32
BF16 = jnp.bfloat16

D_MODEL = 1024
HEAD_DIM = 64
N_HEADS = 8
D_A = N_HEADS * HEAD_DIM
D_B = 256
D_C = 256
D_IN = 3 * D_A + 2 * D_B + D_C
D_FF = 2816
DILATIONS = (1, 4, 16)
SPAN = 128
BLK = 128
CONV_WIDTH = 31
POOL_WINDOWS = (2, 4, 8, 16)
POOL_GROUP = 64
POOL_PREFIX = 15
NUM_BUCKETS = 32
MAX_EXACT = 16
REL_MAX_DIST = 2048
EPS = 1e-6
NEG_INF = -1e30

HEADS_PER_VREG = 128 // HEAD_DIM
N_HEAD_PAIRS = N_HEADS // HEADS_PER_VREG
MXU_WIDTH = 256
HALO = 32
FF_CHUNK = 256
ATTN_GROUP = 32
SA_UNITS_PER_SEQ = 1
SA_SLOTS = 2
LOG2E = math.log2(math.e)
VMEM_LIMIT = 60 * 1024 * 1024


def _bucket_np(dist):
    dist = np.asarray(dist, np.int64)
    ratio = np.log(np.maximum(dist, 1).astype(np.float32) / np.float32(MAX_EXACT))
    large = MAX_EXACT + (ratio / np.float32(math.log(REL_MAX_DIST / MAX_EXACT))
                         * np.float32(NUM_BUCKETS - MAX_EXACT)).astype(np.int32)
    return np.where(dist < MAX_EXACT, dist, np.minimum(large, NUM_BUCKETS - 1)).astype(np.int32)


def _prompt_bucket_table():
    qi = np.arange(BLK)[:, None]
    kk = np.arange(2 * BLK)[None, :]
    dist = BLK + qi - kk
    ok = (dist >= 0) & (dist <= SPAN)
    tabs = []
    for dil in DILATIONS:
        b = _bucket_np(dil * np.clip(dist, 0, SPAN))
        tabs.append(np.where(ok, b, -1))
    return np.stack(tabs).astype(np.int32)


def _sample_tables(wbuf):
    dist = wbuf - np.arange(wbuf)
    count = np.zeros(wbuf, np.int32)
    for dil in DILATIONS:
        count += ((dist % dil == 0) & (dist // dil <= SPAN)).astype(np.int32)
    bucket = np.where(count > 0, _bucket_np(dist), -1).astype(np.int32)
    return bucket[None, :], count.astype(np.float32)[None, :]


def _dot(a, b):
    return jnp.dot(a, b, preferred_element_type=F32)


def _rms(x, g):
    ms = jnp.mean(x * x, axis=-1, keepdims=True)
    return x * lax.rsqrt(ms + EPS) * g


def _silu(x):
    return x * jax.nn.sigmoid(x)


FFN_WEAVE_SLOTS = 3 * (D_FF // FF_CHUNK)


def _weave_plan(units):
    plan = {}
    for n, (wait, work) in enumerate(units):
        chunk = (n * (FFN_WEAVE_SLOTS // 3)) // len(units)
        assert 3 * chunk - 1 not in plan
        plan[3 * chunk - 1] = wait
        plan[3 * chunk + 2] = work
    return plan


def _swiglu_residual(h, g_ref, wgu_ref, wdn_ref, weave=None):
    weave = weave or {}
    xn = _rms(h, g_ref[...]).astype(BF16)
    if -1 in weave:
        weave[-1]()
    acc = None
    for c in range(D_FF // FF_CHUNK):
        lo = c * FF_CHUNK

        def after(n, c=c):
            if 3 * c + n in weave:
                weave[3 * c + n]()

        g = _dot(xn, wgu_ref[:, lo:lo + FF_CHUNK])
        after(0)
        u = _dot(xn, wgu_ref[:, D_FF + lo:D_FF + lo + FF_CHUNK])
        after(1)
        a = (_silu(g) * u).astype(BF16)
        d = _dot(a, wdn_ref[lo:lo + FF_CHUNK, :])
        acc = d if acc is None else acc + d
        after(2)
    return h + 0.5 * acc


def _head_rms(x, bd_ref, g):
    w = bd_ref.shape[0]
    x2 = (x * x).astype(BF16)
    ms = jnp.concatenate([_dot(x2[:, lo:lo + w], bd_ref[...]) for lo in range(0, D_A, w)], axis=1)
    return x * lax.rsqrt(ms + EPS) * g


def _layernorm(x, g, b):
    mu = jnp.mean(x, axis=-1, keepdims=True)
    xc = x - mu
    var = jnp.mean(xc * xc, axis=-1, keepdims=True)
    return xc * lax.rsqrt(var + EPS) * g + b


def _pre_kernel(h_ref, gf_ref, wgu_ref, wdn_ref, gm_ref, win_ref, gq_ref, gk_ref, bd_ref, *rest,
                n_t, q_scale, sa=None):
    weave = None
    if sa is not None:
        sa_in, rest, sa_scratch = rest[:N_SA_INPUTS], rest[N_SA_INPUTS:-3], rest[-3:]
        rest, sa_ref = rest[:-1], rest[-1]
        layer, n_steps, seq_lo, n_host = sa
        weave = _weave_plan(_sample_attn_units(pl.program_id(0), n_steps, layer, seq_lo, n_host,
                                               sa_in, sa_ref, sa_scratch))
    h1_ref, q_ref, k_ref, v_ref, glu_ref, c_ref = rest[:6]
    t_refs = rest[6:]
    assert len(t_refs) == n_t
    h1 = _swiglu_residual(h_ref[...], gf_ref, wgu_ref, wdn_ref, weave)
    h1_ref[...] = h1
    xn = _rms(h1, gm_ref[...]).astype(BF16)
    q = _head_rms(_dot(xn, win_ref[:, 0:D_A]), bd_ref, gq_ref[...]) * q_scale
    k = _head_rms(_dot(xn, win_ref[:, D_A:2 * D_A]), bd_ref, gk_ref[...])
    v = _dot(xn, win_ref[:, 2 * D_A:3 * D_A])
    q_ref[...] = q
    k_ref[...] = k
    v_ref[...] = v
    for t_ref, val in zip(t_refs, (q, k, v)[3 - len(t_refs):]):
        t_ref[...] = val.T.reshape(t_ref.shape)
    o = 3 * D_A
    b_val = _dot(xn, win_ref[:, o:o + D_B])
    b_gate = _dot(xn, win_ref[:, o + D_B:o + 2 * D_B])
    glu_ref[...] = b_val * jax.nn.sigmoid(b_gate)
    c_ref[...] = _dot(xn, win_ref[:, o + 2 * D_B:o + 2 * D_B + D_C])


def _const_spec(shape):
    nd = len(shape)
    return pl.BlockSpec(shape, lambda i, _nd=nd: (0,) * _nd, pipeline_mode=pl.Buffered(1))


def _layer_spec(shape, layer):
    nd = len(shape)
    return pl.BlockSpec((None,) + tuple(shape), lambda i, _nd=nd: (layer,) + (0,) * _nd,
                        pipeline_mode=pl.Buffered(1))


def _pre_call(layer, h, gf, wgu, wdn, gm, win, gq, gk, bd, tm, seq=None, keep=None, sample=None,
              sa_seqs=None):
    t = h.shape[0]
    n_steps = t // tm
    row = lambda w: pl.BlockSpec((tm, w), lambda i: (i, 0))
    in_specs = [row(D_MODEL), _const_spec((1, D_MODEL)), _layer_spec((D_MODEL, 2 * D_FF), layer),
                _layer_spec((D_FF, D_MODEL), layer), _const_spec((1, D_MODEL)),
                _layer_spec((D_MODEL, D_IN), layer), _const_spec((1, D_A)), _const_spec((1, D_A)),
                _const_spec((MXU_WIDTH, MXU_WIDTH))]
    out_specs = [row(D_MODEL), row(D_A), row(D_A), row(D_A), row(D_B), row(D_C)]
    out_shape = [jax.ShapeDtypeStruct((t, w), F32) for w in (D_MODEL, D_A, D_A, D_A, D_B, D_C)]
    if seq is not None:
        per_seq, skip = seq // tm, (seq - keep) // tm
        t_spec = pl.BlockSpec((1, D_A, tm),
                              lambda i: (i // per_seq, 0, jnp.maximum(i % per_seq - skip, 0)))
        out_specs += [t_spec] * 2
        out_shape += [jax.ShapeDtypeStruct((t // seq, D_A, keep), F32)] * 2
    else:
        out_specs += [pl.BlockSpec((D_A, tm), lambda i: (0, i))] * 3
        out_shape += [jax.ShapeDtypeStruct((D_A, t), F32)] * 3
    n_t = len(out_specs) - 6
    scratch, sa, operands = [], None, ()
    if sample is not None:
        sa_in, sa_out, sa_shape, scratch = _sample_attn_specs(sample)
        in_specs += sa_in
        out_specs.append(sa_out)
        out_shape.append(sa_shape)
        sa, operands = (layer, n_steps) + tuple(sa_seqs), tuple(sample)
    return pl.pallas_call(
        functools.partial(_pre_kernel, n_t=n_t, sa=sa,
                          q_scale=HEAD_DIM ** -0.5 * (LOG2E if seq is not None else 1.0)),
        grid=(n_steps,),
        in_specs=in_specs,
        out_specs=out_specs,
        out_shape=out_shape,
        scratch_shapes=scratch,
        compiler_params=pltpu.CompilerParams(dimension_semantics=("arbitrary",),
                                             vmem_limit_bytes=VMEM_LIMIT),
        name="pre_ffn_inproj",
    )(h, gf, wgu, wdn, gm, win, gq, gk, bd, *operands)


def _pool_lane_consts(shape):
    grp = lax.broadcasted_iota(jnp.int32, shape, len(shape) - 1) // POOL_GROUP
    win = jnp.full(shape, POOL_WINDOWS[0], jnp.int32)
    for g in range(1, len(POOL_WINDOWS)):
        win = jnp.where(grp == g, POOL_WINDOWS[g], win)
    return win


def _conv_tail(y, cb_ref, lng_ref, lnb_ref, pw_ref):
    y = _layernorm(y + cb_ref[...], lng_ref[...], lnb_ref[...])
    return _dot(_silu(y).astype(BF16), pw_ref[...])


def _pool_tail(tot, x, cnt, plw_ref, psc_ref):
    d = tot / cnt - x
    return _dot(d.astype(BF16), plw_ref[...]) * psc_ref[...]


def _mix_ffn(h, attn, conv, pool, wout_ref, gf_ref, wgu_ref, wdn_ref, weave=None):
    mix = _dot(attn.astype(BF16), wout_ref[0:D_A, :])
    mix = mix + _dot(conv.astype(BF16), wout_ref[D_A:D_A + D_B, :])
    mix = mix + _dot(pool.astype(BF16), wout_ref[D_A + D_B:D_A + D_B + D_C, :])
    return _swiglu_residual(h + mix, gf_ref, wgu_ref, wdn_ref, weave)


N_SA_INPUTS = 9


def _sample_attn_units(i, n_steps, layer, seq_lo, n_host, in_refs, o_ref, scratch):
    qt_ref, kt_ref, vt_ref, sbias_ref, cnt_ref, rb0_ref, init_ref, ck_hbm, cv_hbm = in_refs
    kbuf, vbuf, sems = scratch
    n_seq = qt_ref.shape[1]
    units = SA_UNITS_PER_SEQ * n_host // n_steps
    assert units * n_steps == SA_UNITS_PER_SEQ * n_host and units % SA_SLOTS == 0
    heads = N_HEADS // SA_UNITS_PER_SEQ

    def seq_of(step, j):
        return seq_lo + step * (units // SA_UNITS_PER_SEQ) + j // SA_UNITS_PER_SEQ

    def copies(step, j):
        g, slot = j % SA_UNITS_PER_SEQ, j % SA_SLOTS
        return [pltpu.make_async_copy(hbm.at[layer, seq_of(step, j), pl.ds(g * heads, heads)],
                                      buf.at[slot], sems.at[n, slot])
                for n, (hbm, buf) in enumerate(((ck_hbm, kbuf), (cv_hbm, vbuf)))]

    def start(step, j):
        for cp in copies(step, j):
            cp.start()

    @pl.when(i == 0)
    def _():
        o_ref[...] = init_ref[...]
        for j in range(SA_SLOTS):
            start(i, j)

    lane = lax.broadcasted_iota(jnp.int32, (D_A, n_seq), 1)
    head = lambda a, h: a[h * HEAD_DIM:(h + 1) * HEAD_DIM]
    cols = {}

    def wait(j):
        for cp in copies(i, j):
            cp.wait()

    def refill(j):
        nxt = j + SA_SLOTS
        if nxt < units:
            start(i, nxt)
        else:
            @pl.when(i + 1 < n_steps)
            def _():
                start(i + 1, nxt - units)

    def unit_pieces(j):
        g, slot = j % SA_UNITS_PER_SEQ, j % SA_SLOTS
        h0 = g * heads
        st = {"rows": [], "new": [], "outs": []}

        def keys(hh_lo, hh_hi):
            if hh_lo == 0:
                st["s"] = seq_of(i, j)
                if j // SA_UNITS_PER_SEQ not in cols:
                    sel = lane == st["s"]
                    cols[j // SA_UNITS_PER_SEQ] = [
                        jnp.sum(jnp.where(sel, r[...], 0.0), axis=1, keepdims=True)
                        for r in (qt_ref, kt_ref, vt_ref)]
            qc, kc, _ = cols[j // SA_UNITS_PER_SEQ]
            for hh in range(hh_lo, hh_hi):
                st["rows"].append(jnp.sum(kbuf[slot, hh] * head(qc, h0 + hh), axis=0, keepdims=True))
                st["new"].append(jnp.sum(head(qc, h0 + hh) * head(kc, h0 + hh), axis=0,
                                         keepdims=True))

        def softmax():
            lg = jnp.concatenate(st["rows"], axis=0) + sbias_ref[h0:h0 + heads, :]
            lg0 = jnp.concatenate(st["new"], axis=0) + rb0_ref[h0:h0 + heads, :]
            m = jnp.maximum(jnp.max(lg, axis=1, keepdims=True), lg0)
            st["p"] = jnp.exp(lg - m) * cnt_ref[...]
            st["p0"] = len(DILATIONS) * jnp.exp(lg0 - m)
            st["den"] = jnp.sum(st["p"], axis=1, keepdims=True) + st["p0"]

        def values(hh_lo, hh_hi):
            vc = cols[j // SA_UNITS_PER_SEQ][2]
            for hh in range(hh_lo, hh_hi):
                num = jnp.sum(vbuf[slot, hh] * st["p"][hh:hh + 1, :], axis=1, keepdims=True)
                num = num + st["p0"][hh:hh + 1] * head(vc, h0 + hh)
                st["outs"].append(num / st["den"][hh:hh + 1])
            if hh_hi == heads:
                r0, r1 = h0 * HEAD_DIM, (h0 + heads) * HEAD_DIM
                mine = lax.broadcasted_iota(jnp.int32, (r1 - r0, n_seq), 1) == st["s"]
                o_ref[r0:r1, :] = jnp.where(mine, jnp.concatenate(st["outs"], axis=0),
                                            o_ref[r0:r1, :])
                refill(j)

        def whole():
            keys(0, heads)
            softmax()
            values(0, heads)

        return [functools.partial(wait, j), whole]

    return [unit_pieces(j) for j in range(units)]


def _prompt_mixer_pieces(tile, g_ref, c_ref, gx, cx, conv_dst, pool_dst, mixer_w, tm):
    cw_ref, cb_ref, lng_ref, lnb_ref, pw_ref, plw_ref, psc_ref = mixer_w
    ext = tm + 8
    state = {}

    def load():
        keep = (tile > 0).astype(F32)
        gx[0:HALO, :] = gx[0:HALO, :] * keep
        gx[HALO:HALO + tm, :] = g_ref[...]
        cx[0:HALO, :] = cx[0:HALO, :] * keep
        cx[HALO:HALO + tm, :] = c_ref[...]

    def taps(b_lo, b_hi):
        y = state.get("y")
        for b in range(b_lo, b_hi):
            part = None
            for a in range(-(-CONV_WIDTH // 8)):
                lag = 8 * a + b
                if lag >= CONV_WIDTH:
                    continue
                lo = HALO - 8 - 8 * a
                term = cw_ref[CONV_WIDTH - 1 - lag:CONV_WIDTH - lag, :] * gx[lo:lo + ext, :]
                part = term if part is None else part + term
            if b:
                part = pltpu.roll(part, b, 0)
            y = part if y is None else y + part
        state["y"] = y

    def conv_tail():
        conv_dst[...] = _conv_tail(state["y"][8:], cb_ref, lng_ref, lnb_ref, pw_ref).astype(BF16)

    def pool():
        x = c_ref[...]
        win = _pool_lane_consts((tm, D_C))
        run = cx[...]
        tot = jnp.zeros_like(x)
        prev_w = 1
        for w in POOL_WINDOWS:
            assert w == 2 * prev_w
            run = run + pltpu.roll(run, prev_w, 0)
            tot = jnp.where(win == w, run[HALO:], tot)
            prev_w = w
        pos = tile * tm + lax.broadcasted_iota(jnp.int32, (tm, D_C), 0)
        cnt = jnp.minimum(win, pos + 1).astype(F32)
        pool_dst[...] = _pool_tail(tot, x, cnt, plw_ref, psc_ref).astype(BF16)

    def first():
        load()
        taps(0, 1)

    return [first] + [functools.partial(taps, b, b + 1) for b in range(1, 8)] + [conv_tail, pool]


def _post_prompt_kernel(h_ref, attn_ref, g_ref, gh_ref, c_ref, ch_ref,
                        cw_ref, cb_ref, lng_ref, lnb_ref, pw_ref, plw_ref, psc_ref,
                        wout_ref, gf_ref, wgu_ref, wdn_ref, *rest,
                        tm, per_seq, layer, n_steps, sa_seqs):
    sa_in, outs_scratch, sa_scratch = rest[:N_SA_INPUTS], rest[N_SA_INPUTS:-3], rest[-3:]
    out_ref, sa_ref, gx, cx, conv_t, pool_t = outs_scratch
    mixer_w = (cw_ref, cb_ref, lng_ref, lnb_ref, pw_ref, plw_ref, psc_ref)
    i = pl.program_id(0)
    units = _sample_attn_units(i, n_steps, layer, *sa_seqs, sa_in, sa_ref, sa_scratch)
    gx[0:HALO, :] = gh_ref[...]
    cx[0:HALO, :] = ch_ref[...]
    for piece in _prompt_mixer_pieces(i % per_seq, g_ref, c_ref, gx, cx, conv_t, pool_t, mixer_w, tm):
        piece()
    out_ref[...] = _mix_ffn(h_ref[...], attn_ref[...], conv_t[...], pool_t[...],
                            wout_ref, gf_ref, wgu_ref, wdn_ref, _weave_plan(units))


def _post_sample_kernel(h_ref, attn_ref, g_ref, c_ref, cc_ref, cp_ref,
                        cw_ref, cb_ref, lng_ref, lnb_ref, pw_ref, plw_ref, psc_ref,
                        wout_ref, gf_ref, wgu_ref, wdn_ref, out_ref):
    n_hist = CONV_WIDTH - 1
    y = g_ref[...] * cw_ref[n_hist:n_hist + 1, :]
    for j in range(n_hist):
        y = y + cc_ref[0, j] * cw_ref[j:j + 1, :]
    conv = _conv_tail(y, cb_ref, lng_ref, lnb_ref, pw_ref)

    x = c_ref[...]
    win = _pool_lane_consts(x.shape)
    run = x
    tot = jnp.zeros_like(x)
    prev_w = 1
    for w in POOL_WINDOWS:
        for sft in range(prev_w, w):
            run = run + cp_ref[0, POOL_PREFIX - sft]
        tot = jnp.where(win == w, run, tot)
        prev_w = w
    pool = _pool_tail(tot, x, win.astype(F32), plw_ref, psc_ref)

    out_ref[...] = _mix_ffn(h_ref[...], attn_ref[...].T, conv, pool, wout_ref, gf_ref, wgu_ref, wdn_ref)


def _mixer_weight_specs(layer):
    return [_const_spec((CONV_WIDTH, D_B)), _const_spec((1, D_B)), _const_spec((1, D_B)),
            _const_spec((1, D_B)), _const_spec((D_B, D_B)), _const_spec((D_C, D_C)),
            _const_spec((1, D_C)),
            _layer_spec((D_MODEL, D_MODEL), layer), _const_spec((1, D_MODEL)),
            _layer_spec((D_MODEL, 2 * D_FF), layer), _layer_spec((D_FF, D_MODEL), layer)]


def _sample_attn_specs(sample):
    assert len(sample) == N_SA_INPUTS
    n_seq, wbuf = sample[0].shape[1], sample[-1].shape[-1]
    heads = N_HEADS // SA_UNITS_PER_SEQ
    cst = lambda shape: pl.BlockSpec(shape, lambda i: (0,) * len(shape))
    hbm = pl.BlockSpec(memory_space=pl.ANY)
    in_specs = [cst((D_A, n_seq))] * 3 + [cst((N_HEADS, wbuf)), cst((1, wbuf)), cst((N_HEADS, 1)),
                                          cst((D_A, n_seq)), hbm, hbm]
    scratch = [pltpu.VMEM((SA_SLOTS, heads, HEAD_DIM, wbuf), F32),
               pltpu.VMEM((SA_SLOTS, heads, HEAD_DIM, wbuf), F32),
               pltpu.SemaphoreType.DMA((2, SA_SLOTS))]
    return in_specs, cst((D_A, n_seq)), jax.ShapeDtypeStruct((D_A, n_seq), F32), scratch


def _post_prompt_call(layer, h, attn, glu, c, mixw, postw, sample, sa_seqs, tm, seq):
    t = h.shape[0]
    n_steps = t // tm
    per = tm // HALO
    row = lambda w: pl.BlockSpec((tm, w), lambda i: (i, 0))
    halo = lambda w: pl.BlockSpec((HALO, w), lambda i: (jnp.maximum(i * per - 1, 0), 0))
    sa_in, sa_out, sa_shape, sa_scratch = _sample_attn_specs(sample)
    return pl.pallas_call(
        functools.partial(_post_prompt_kernel, tm=tm, per_seq=seq // tm, layer=layer,
                          n_steps=n_steps, sa_seqs=sa_seqs),
        grid=(n_steps,),
        in_specs=[row(D_MODEL), row(D_A), row(D_B), halo(D_B), row(D_C), halo(D_C)]
        + _mixer_weight_specs(layer) + sa_in,
        out_specs=[row(D_MODEL), sa_out],
        out_shape=[jax.ShapeDtypeStruct((t, D_MODEL), F32), sa_shape],
        scratch_shapes=[pltpu.VMEM((HALO + tm, D_B), F32), pltpu.VMEM((HALO + tm, D_C), F32),
                        pltpu.VMEM((tm, D_B), BF16), pltpu.VMEM((tm, D_C), BF16)] + sa_scratch,
        compiler_params=pltpu.CompilerParams(dimension_semantics=("arbitrary",),
                                             vmem_limit_bytes=VMEM_LIMIT),
        name="post_prompt",
    )(h, attn, glu, glu, c, c, *mixw, *postw, *sample)


def _post_sample_call(layer, h, attn, glu, c, cct, cpt, mixw, postw):
    t = h.shape[0]
    full = lambda w: pl.BlockSpec((t, w), lambda i: (0, 0))
    return pl.pallas_call(
        _post_sample_kernel,
        grid=(1,),
        in_specs=[full(D_MODEL), pl.BlockSpec((D_A, t), lambda i: (0, 0)), full(D_B), full(D_C),
                  pl.BlockSpec((1, CONV_WIDTH - 1, t, D_B), lambda i: (layer, 0, 0, 0)),
                  pl.BlockSpec((1, POOL_PREFIX, t, D_C), lambda i: (layer, 0, 0, 0))]
        + _mixer_weight_specs(layer),
        out_specs=full(D_MODEL),
        out_shape=jax.ShapeDtypeStruct((t, D_MODEL), F32),
        compiler_params=pltpu.CompilerParams(dimension_semantics=("arbitrary",),
                                             vmem_limit_bytes=VMEM_LIMIT),
        name="post_sample",
    )(h, attn, glu, c, cct, cpt, *mixw, *postw)


def _bias_prompt_kernel(rb_ref, tbl_ref, out_ref):
    tbl = tbl_ref[0]
    for h in range(N_HEADS):
        def body(b, acc, h=h):
            return jnp.where(tbl == b, rb_ref[b, h], acc)

        acc = lax.fori_loop(0, NUM_BUCKETS, body, jnp.zeros(tbl.shape, F32))
        out_ref[0, h] = jnp.where(tbl < 0, NEG_INF, acc * LOG2E)


def _bias_prompt_call(rel_bias):
    tbl = jnp.asarray(_prompt_bucket_table())
    nbr = len(DILATIONS)
    return pl.pallas_call(
        _bias_prompt_kernel,
        grid=(nbr,),
        in_specs=[pl.BlockSpec(memory_space=pltpu.SMEM),
                  pl.BlockSpec((1, BLK, 2 * BLK), lambda br: (br, 0, 0))],
        out_specs=pl.BlockSpec((1, N_HEADS, BLK, 2 * BLK), lambda br: (br, 0, 0, 0)),
        out_shape=jax.ShapeDtypeStruct((nbr, N_HEADS, BLK, 2 * BLK), F32),
        name="bias_prompt",
    )(rel_bias, tbl)


def _bias_sample_kernel(rb_ref, tbl_ref, out_ref):
    tbl = tbl_ref[...]
    for h in range(N_HEADS):
        def body(b, acc, h=h):
            return jnp.where(tbl == b, rb_ref[b, h], acc)

        acc = lax.fori_loop(0, NUM_BUCKETS, body, jnp.zeros(tbl.shape, F32))
        out_ref[h:h + 1, :] = jnp.where(tbl < 0, NEG_INF, acc)


def _bias_sample_call(rel_bias, bucket):
    return pl.pallas_call(
        _bias_sample_kernel,
        in_specs=[pl.BlockSpec(memory_space=pltpu.SMEM),
                  pl.BlockSpec(bucket.shape, lambda: (0, 0))],
        out_shape=jax.ShapeDtypeStruct((N_HEADS, bucket.shape[1]), F32),
        name="bias_sample",
    )(rel_bias, bucket)


def _rows(start, size, dil):
    return pl.ds(start, size) if dil == 1 else pl.ds(start, size, stride=dil)


def _attn_prompt_kernel(q_ref, k_ref, v_ref, bias_ref, o_ref, m_acc, s_acc, n_acc, *, seq):
    lane_lo = lax.broadcasted_iota(jnp.int32, (BLK, 128), 1) < HEAD_DIM

    def group(br, dil, starts, n_keys, order):
        bias = bias_ref[br, 0, :, 2 * BLK - n_keys:]
        logits, values = [], []
        for q_start, k_start in starts:
            q = q_ref[0, _rows(q_start, BLK, dil), :]
            qs = jnp.concatenate([jnp.where(lane_lo, q, 0.0), jnp.where(lane_lo, 0.0, q)], axis=0)
            kb = k_ref[0, _rows(k_start, n_keys, dil), :].astype(BF16)
            values.append(v_ref[0, _rows(k_start, n_keys, dil), :].astype(BF16))
            logits.append(lax.dot_general(qs.astype(BF16), kb, (((1,), (1,)), ((), ())),
                                          preferred_element_type=F32) + bias)
        stats = []
        ones = jnp.ones((n_keys, 128), BF16)
        for lg, vb in zip(logits, values):
            m = jnp.max(lg, axis=-1, keepdims=True)
            p = jnp.exp2(lg - m)
            pv = _dot(p.astype(BF16), jnp.concatenate([vb, ones], axis=1))
            num, s = pv[:, :128], pv[:, 128:]
            stats.append((jnp.where(lane_lo, m[:BLK], m[BLK:]),
                          jnp.where(lane_lo, s[:BLK], s[BLK:]),
                          jnp.where(lane_lo, num[:BLK], num[BLK:])))
        rows = [_rows(q_start, BLK, dil) for q_start, _ in starts]
        last = order == len(DILATIONS) - 1
        if order == 0:
            for r, (mm, ss, num) in zip(rows, stats):
                m_acc[r, :] = mm
                s_acc[r, :] = ss
                n_acc[r, :] = num
            return
        old = [(m_acc[r, :], s_acc[r, :], n_acc[r, :]) for r in rows]
        for r, (mm, ss, num), (m_old, s_old, n_old) in zip(rows, stats, old):
            m_new = jnp.maximum(m_old, mm)
            a = jnp.exp2(m_old - m_new)
            b = jnp.exp2(mm - m_new)
            s_new = a * s_old + b * ss
            n_new = a * n_old + b * num
            if last:
                o_ref[0, r, :] = n_new / s_new
            else:
                m_acc[r, :] = m_new
                s_acc[r, :] = s_new
                n_acc[r, :] = n_new

    for order, (br, dil) in enumerate(reversed(list(enumerate(DILATIONS)))):
        nb = seq // dil // BLK

        def starts_of(u, first, dil=dil):
            if first:
                return u, u
            lb = u // dil + 1
            r = u % dil
            return r + dil * BLK * lb, r + dil * BLK * (lb - 1)

        for first, n_units in ((True, dil), (False, dil * (nb - 1))):
            n_keys = BLK if first else 2 * BLK
            rem = n_units % ATTN_GROUP
            if rem:
                group(br, dil, [starts_of(u, first) for u in range(rem)], n_keys, order)

            def body(g, carry, br=br, dil=dil, order=order, first=first, rem=rem, n_keys=n_keys,
                     starts_of=starts_of):
                u0 = rem + g * ATTN_GROUP
                group(br, dil, [starts_of(u0 + j, first) for j in range(ATTN_GROUP)], n_keys, order)
                return carry

            lax.fori_loop(0, n_units // ATTN_GROUP, body, 0)


def _attn_prompt_call(q, k, v, bias2):
    bsz, seq, _ = q.shape
    blk = pl.BlockSpec((1, seq, 128), lambda b, hp: (b, 0, hp))
    return pl.pallas_call(
        functools.partial(_attn_prompt_kernel, seq=seq),
        grid=(bsz, N_HEAD_PAIRS),
        in_specs=[blk, blk, blk,
                  pl.BlockSpec((len(DILATIONS), 1, HEADS_PER_VREG * BLK, 2 * BLK),
                               lambda b, hp: (0, hp, 0, 0))],
        out_specs=blk,
        out_shape=jax.ShapeDtypeStruct((bsz, seq, D_A), F32),
        scratch_shapes=[pltpu.VMEM((seq, 128), F32)] * 3,
        compiler_params=pltpu.CompilerParams(dimension_semantics=("arbitrary", "arbitrary"),
                                             vmem_limit_bytes=VMEM_LIMIT),
        name="attn_prompt",
    )(q, k, v, bias2)


def _block_diag(blocks):
    n, r, c = blocks.shape
    eye = jnp.eye(n, dtype=blocks.dtype)
    return (eye[:, None, :, None] * blocks[:, :, None, :]).reshape(n * r, n * c)


def kernel(x_prompt, x_sample, cache_attn_k, cache_attn_v, cache_conv, cache_pool, rel_bias, g_ffn1, w_ffn1_gu, w_ffn1_down, g_mix, w_in, g_q, g_k, conv_w, conv_b, conv_ln_g, conv_ln_b, conv_pw, pool_w, pool_scale, w_out, g_ffn2, w_ffn2_gu, w_ffn2_down):
    bsz, seq, _ = x_prompt.shape
    nseq = x_sample.shape[0]
    depth = g_ffn1.shape[0]
    wbuf = cache_attn_k.shape[2]
    assert x_sample.shape[1] == 1 and wbuf == DILATIONS[-1] * SPAN and nseq == 128
    tm_prompt, tm_post, tm_sample = 512, 512, nseq

    head_of_lane = np.arange(MXU_WIDTH) // HEAD_DIM
    bd = jnp.asarray((head_of_lane[:, None] == head_of_lane[None, :]) / HEAD_DIM, BF16)
    bias_p = _bias_prompt_call(rel_bias).reshape(
        len(DILATIONS), N_HEAD_PAIRS, HEADS_PER_VREG * BLK, 2 * BLK)
    s_bucket, s_count = _sample_tables(wbuf)
    bias_s = _bias_sample_call(rel_bias, jnp.asarray(s_bucket))
    s_count = jnp.asarray(s_count)
    rb0 = rel_bias[0].reshape(N_HEADS, 1)
    ck = jnp.transpose(cache_attn_k, (0, 1, 3, 4, 2))
    cv = jnp.transpose(cache_attn_v, (0, 1, 3, 4, 2))
    cct = jnp.transpose(cache_conv, (0, 2, 1, 3))
    cpt = jnp.transpose(cache_pool, (0, 2, 1, 3))

    hp = x_prompt.reshape(bsz * seq, D_MODEL)
    hs = x_sample.reshape(nseq, D_MODEL)
    outs = {n: [] for n in ("pk", "pv", "pc", "pp", "sk", "sv", "sc", "sp")}
    wgu1, wdn1, win = (w.astype(BF16) for w in (w_ffn1_gu, w_ffn1_down, w_in))
    wgu2, wdn2, wout = (w.astype(BF16) for w in (w_ffn2_gu, w_ffn2_down, w_out))
    for l in range(depth):
        r1 = lambda a: a[l].reshape(1, -1)
        gq = jnp.tile(g_q[l], N_HEADS).reshape(1, D_A)
        gk = jnp.tile(g_k[l], N_HEADS).reshape(1, D_A)
        pw = conv_pw[l].astype(BF16)
        plw = _block_diag(pool_w[l]).astype(BF16)
        mixw = (conv_w[l], r1(conv_b), r1(conv_ln_g), r1(conv_ln_b), pw, plw, r1(pool_scale))
        pre_w = (r1(g_ffn1), wgu1, wdn1, r1(g_mix), win, gq, gk, bd)
        post_w = (wout, r1(g_ffn2), wgu2, wdn2)

        keep = min(wbuf, seq)
        hs, _, sk, sv, sglu, sc, sq_t, sk_t, sv_t = _pre_call(l, hs, *pre_w, tm=tm_sample)
        hp, q, k, v, glu, c, k_t, v_t = _pre_call(l, hp, *pre_w, tm=tm_prompt, seq=seq, keep=keep)

        sq = lambda a: a.reshape(bsz, seq, a.shape[-1])
        attn = _attn_prompt_call(sq(q), sq(k), sq(v), bias_p)
        sample = (sq_t, sk_t, sv_t, bias_s, s_count, rb0, jnp.zeros((D_A, nseq), F32), ck, cv)
        hp, sattn_t = _post_prompt_call(
            l, hp, attn.reshape(bsz * seq, D_A), glu, c, mixw, post_w,
            sample, (0, nseq), tm=tm_post, seq=seq)
        to_cache = lambda a: a.reshape(bsz, N_HEADS, HEAD_DIM, keep).transpose(0, 3, 1, 2)
        outs["pk"].append(to_cache(k_t))
        outs["pv"].append(to_cache(v_t))
        outs["pc"].append(sq(glu)[:, seq - (CONV_WIDTH - 1):])
        outs["pp"].append(sq(c)[:, seq - POOL_PREFIX:])

        hs = _post_sample_call(l, hs, sattn_t, sglu, sc, cct, cpt, mixw, post_w)
        outs["sk"].append(sk.reshape(nseq, 1, N_HEADS, HEAD_DIM))
        outs["sv"].append(sv.reshape(nseq, 1, N_HEADS, HEAD_DIM))
        outs["sc"].append(jnp.concatenate([cache_conv[l][:, 1:], sglu[:, None, :]], axis=1))
        outs["sp"].append(jnp.concatenate([cache_pool[l][:, 1:], sc[:, None, :]], axis=1))

    st = lambda n: jnp.stack(outs[n])
    return (hp.reshape(bsz, seq, D_MODEL), hs.reshape(nseq, 1, D_MODEL),
            st("pk"), st("pv"), st("pc"), st("pp"), st("sk"), st("sv"), st("sc"), st("sp"))
```

```python
import functools
import math

import numpy as np
import jax
import jax.numpy as jnp
from jax import lax
from jax.experimental import pallas as pl
from jax.experimental.pallas import tpu as pltpu

F32 = jnp.float32
BF16 = jnp.bfloat16

D_MODEL = 1024
HEAD_DIM = 64
N_HEADS = 8
D_A = N_HEADS * HEAD_DIM
D_B = 256
D_C = 256
D_IN = 3 * D_A + 2 * D_B + D_C
D_FF = 2816
DILATIONS = (1, 4, 16)
SPAN = 128
BLK = 128
CONV_WIDTH = 31
POOL_WINDOWS = (2, 4, 8, 16)
POOL_GROUP = 64
POOL_PREFIX = 15
NUM_BUCKETS = 32
MAX_EXACT = 16
REL_MAX_DIST = 2048
EPS = 1e-6
NEG_INF = -1e30

HEADS_PER_VREG = 128 // HEAD_DIM
N_HEAD_PAIRS = N_HEADS // HEADS_PER_VREG
MXU_WIDTH = 256
HALO = 32
FF_CHUNK = 256
ATTN_GROUP = 32
SA_UNITS_PER_SEQ = 1
SA_SLOTS = 2
LOG2E = math.log2(math.e)
VMEM_LIMIT = 60 * 1024 * 1024


def _bucket_np(dist):
    dist = np.asarray(dist, np.int64)
    ratio = np.log(np.maximum(dist, 1).astype(np.float32) / np.float32(MAX_EXACT))
    large = MAX_EXACT + (ratio / np.float32(math.log(REL_MAX_DIST / MAX_EXACT))
                         * np.float32(NUM_BUCKETS - MAX_EXACT)).astype(np.int32)
    return np.where(dist < MAX_EXACT, dist, np.minimum(large, NUM_BUCKETS - 1)).astype(np.int32)


def _prompt_bucket_table():
    qi = np.arange(BLK)[:, None]
    kk = np.arange(2 * BLK)[None, :]
    dist = BLK + qi - kk
    ok = (dist >= 0) & (dist <= SPAN)
    tabs = []
    for dil in DILATIONS:
        b = _bucket_np(dil * np.clip(dist, 0, SPAN))
        tabs.append(np.where(ok, b, -1))
    return np.stack(tabs).astype(np.int32)


def _sample_tables(wbuf):
    dist = wbuf - np.arange(wbuf)
    count = np.zeros(wbuf, np.int32)
    for dil in DILATIONS:
        count += ((dist % dil == 0) & (dist // dil <= SPAN)).astype(np.int32)
    bucket = np.where(count > 0, _bucket_np(dist), -1).astype(np.int32)
    return bucket[None, :], count.astype(np.float32)[None, :]


def _dot(a, b):
    return jnp.dot(a, b, preferred_element_type=F32)


def _rms(x, g):
    ms = jnp.mean(x * x, axis=-1, keepdims=True)
    return x * lax.rsqrt(ms + EPS) * g


def _silu(x):
    return x * jax.nn.sigmoid(x)


FFN_WEAVE_SLOTS = 3 * (D_FF // FF_CHUNK)


def _weave_plan(units):
    plan = {}
    for n, (wait, work) in enumerate(units):
        chunk = (n * (FFN_WEAVE_SLOTS // 3)) // len(units)
        assert 3 * chunk - 1 not in plan
        plan[3 * chunk - 1] = wait
        plan[3 * chunk + 2] = work
    return plan


def _swiglu_residual(h, g_ref, wgu_ref, wdn_ref, weave=None):
    weave = weave or {}
    xn = _rms(h, g_ref[...]).astype(BF16)
    if -1 in weave:
        weave[-1]()
    acc = None
    for c in range(D_FF // FF_CHUNK):
        lo = c * FF_CHUNK

        def after(n, c=c):
            if 3 * c + n in weave:
                weave[3 * c + n]()

        g = _dot(xn, wgu_ref[:, lo:lo + FF_CHUNK])
        after(0)
        u = _dot(xn, wgu_ref[:, D_FF + lo:D_FF + lo + FF_CHUNK])
        after(1)
        a = (_silu(g) * u).astype(BF16)
        d = _dot(a, wdn_ref[lo:lo + FF_CHUNK, :])
        acc = d if acc is None else acc + d
        after(2)
    return h + 0.5 * acc


def _head_rms(x, bd_ref, g):
    w = bd_ref.shape[0]
    x2 = (x * x).astype(BF16)
    ms = jnp.concatenate([_dot(x2[:, lo:lo + w], bd_ref[...]) for lo in range(0, D_A, w)], axis=1)
    return x * lax.rsqrt(ms + EPS) * g


def _layernorm(x, g, b):
    mu = jnp.mean(x, axis=-1, keepdims=True)
    xc = x - mu
    var = jnp.mean(xc * xc, axis=-1, keepdims=True)
    return xc * lax.rsqrt(var + EPS) * g + b


def _pre_kernel(h_ref, gf_ref, wgu_ref, wdn_ref, gm_ref, win_ref, gq_ref, gk_ref, bd_ref, *rest,
                n_t, q_scale, sa=None):
    weave = None
    if sa is not None:
        sa_in, rest, sa_scratch = rest[:N_SA_INPUTS], rest[N_SA_INPUTS:-3], rest[-3:]
        rest, sa_ref = rest[:-1], rest[-1]
        layer, n_steps, seq_lo, n_host = sa
        weave = _weave_plan(_sample_attn_units(pl.program_id(0), n_steps, layer, seq_lo, n_host,
                                               sa_in, sa_ref, sa_scratch))
    h1_ref, q_ref, k_ref, v_ref, glu_ref, c_ref = rest[:6]
    t_refs = rest[6:]
    assert len(t_refs) == n_t
    h1 = _swiglu_residual(h_ref[...], gf_ref, wgu_ref, wdn_ref, weave)
    h1_ref[...] = h1
    xn = _rms(h1, gm_ref[...]).astype(BF16)
    q = _head_rms(_dot(xn, win_ref[:, 0:D_A]), bd_ref, gq_ref[...]) * q_scale
    k = _head_rms(_dot(xn, win_ref[:, D_A:2 * D_A]), bd_ref, gk_ref[...])
    v = _dot(xn, win_ref[:, 2 * D_A:3 * D_A])
    q_ref[...] = q
    k_ref[...] = k
    v_ref[...] = v
    for t_ref, val in zip(t_refs, (q, k, v)[3 - len(t_refs):]):
        t_ref[...] = val.T.reshape(t_ref.shape)
    o = 3 * D_A
    b_val = _dot(xn, win_ref[:, o:o + D_B])
    b_gate = _dot(xn, win_ref[:, o + D_B:o + 2 * D_B])
    glu_ref[...] = b_val * jax.nn.sigmoid(b_gate)
    c_ref[...] = _dot(xn, win_ref[:, o + 2 * D_B:o + 2 * D_B + D_C])


def _const_spec(shape):
    nd = len(shape)
    return pl.BlockSpec(shape, lambda i, _nd=nd: (0,) * _nd, pipeline_mode=pl.Buffered(1))


def _layer_spec(shape, layer):
    nd = len(shape)
    return pl.BlockSpec((None,) + tuple(shape), lambda i, _nd=nd: (layer,) + (0,) * _nd,
                        pipeline_mode=pl.Buffered(1))


def _pre_call(layer, h, gf, wgu, wdn, gm, win, gq, gk, bd, tm, seq=None, keep=None, sample=None,
              sa_seqs=None):
    t = h.shape[0]
    n_steps = t // tm
    row = lambda w: pl.BlockSpec((tm, w), lambda i: (i, 0))
    in_specs = [row(D_MODEL), _const_spec((1, D_MODEL)), _layer_spec((D_MODEL, 2 * D_FF), layer),
                _layer_spec((D_FF, D_MODEL), layer), _const_spec((1, D_MODEL)),
                _layer_spec((D_MODEL, D_IN), layer), _const_spec((1, D_A)), _const_spec((1, D_A)),
                _const_spec((MXU_WIDTH, MXU_WIDTH))]
    out_specs = [row(D_MODEL), row(D_A), row(D_A), row(D_A), row(D_B), row(D_C)]
    out_shape = [jax.ShapeDtypeStruct((t, w), F32) for w in (D_MODEL, D_A, D_A, D_A, D_B, D_C)]
    if seq is not None:
        per_seq, skip = seq // tm, (seq - keep) // tm
        t_spec = pl.BlockSpec((1, D_A, tm),
                              lambda i: (i // per_seq, 0, jnp.maximum(i % per_seq - skip, 0)))
        out_specs += [t_spec] * 2
        out_shape += [jax.ShapeDtypeStruct((t // seq, D_A, keep), F32)] * 2
    else:
        out_specs += [pl.BlockSpec((D_A, tm), lambda i: (0, i))] * 3
        out_shape += [jax.ShapeDtypeStruct((D_A, t), F32)] * 3
    n_t = len(out_specs) - 6
    scratch, sa, operands = [], None, ()
    if sample is not None:
        sa_in, sa_out, sa_shape, scratch = _sample_attn_specs(sample)
        in_specs += sa_in
        out_specs.append(sa_out)
        out_shape.append(sa_shape)
        sa, operands = (layer, n_steps) + tuple(sa_seqs), tuple(sample)
    return pl.pallas_call(
        functools.partial(_pre_kernel, n_t=n_t, sa=sa,
                          q_scale=HEAD_DIM ** -0.5 * (LOG2E if seq is not None else 1.0)),
        grid=(n_steps,),
        in_specs=in_specs,
        out_specs=out_specs,
        out_shape=out_shape,
        scratch_shapes=scratch,
        compiler_params=pltpu.CompilerParams(dimension_semantics=("arbitrary",),
                                             vmem_limit_bytes=VMEM_LIMIT),
        name="pre_ffn_inproj",
    )(h, gf, wgu, wdn, gm, win, gq, gk, bd, *operands)


def _pool_lane_consts(shape):
    grp = lax.broadcasted_iota(jnp.int32, shape, len(shape) - 1) // POOL_GROUP
    win = jnp.full(shape, POOL_WINDOWS[0], jnp.int32)
    for g in range(1, len(POOL_WINDOWS)):
        win = jnp.where(grp == g, POOL_WINDOWS[g], win)
    return win


def _conv_tail(y, cb_ref, lng_ref, lnb_ref, pw_ref):
    y = _layernorm(y + cb_ref[...], lng_ref[...], lnb_ref[...])
    return _dot(_silu(y).astype(BF16), pw_ref[...])


def _pool_tail(tot, x, cnt, plw_ref, psc_ref):
    d = tot / cnt - x
    return _dot(d.astype(BF16), plw_ref[...]) * psc_ref[...]


def _mix_ffn(h, attn, conv, pool, wout_ref, gf_ref, wgu_ref, wdn_ref, weave=None):
    mix = _dot(attn.astype(BF16), wout_ref[0:D_A, :])
    mix = mix + _dot(conv.astype(BF16), wout_ref[D_A:D_A + D_B, :])
    mix = mix + _dot(pool.astype(BF16), wout_ref[D_A + D_B:D_A + D_B + D_C, :])
    return _swiglu_residual(h + mix, gf_ref, wgu_ref, wdn_ref, weave)


N_SA_INPUTS = 9


def _sample_attn_units(i, n_steps, layer, seq_lo, n_host, in_refs, o_ref, scratch):
    qt_ref, kt_ref, vt_ref, sbias_ref, cnt_ref, rb0_ref, init_ref, ck_hbm, cv_hbm = in_refs
    kbuf, vbuf, sems = scratch
    n_seq = qt_ref.shape[1]
    units = SA_UNITS_PER_SEQ * n_host // n_steps
    assert units * n_steps == SA_UNITS_PER_SEQ * n_host and units % SA_SLOTS == 0
    heads = N_HEADS // SA_UNITS_PER_SEQ

    def seq_of(step, j):
        return seq_lo + step * (units // SA_UNITS_PER_SEQ) + j // SA_UNITS_PER_SEQ

    def copies(step, j):
        g, slot = j % SA_UNITS_PER_SEQ, j % SA_SLOTS
        return [pltpu.make_async_copy(hbm.at[layer, seq_of(step, j), pl.ds(g * heads, heads)],
                                      buf.at[slot], sems.at[n, slot])
                for n, (hbm, buf) in enumerate(((ck_hbm, kbuf), (cv_hbm, vbuf)))]

    def start(step, j):
        for cp in copies(step, j):
            cp.start(priority=1)

    @pl.when(i == 0)
    def _():
        o_ref[...] = init_ref[...]
        for j in range(SA_SLOTS):
            start(i, j)

    lane = lax.broadcasted_iota(jnp.int32, (D_A, n_seq), 1)
    head = lambda a, h: a[h * HEAD_DIM:(h + 1) * HEAD_DIM]
    cols = {}

    def wait(j):
        for cp in copies(i, j):
            cp.wait()

    def refill(j):
        nxt = j + SA_SLOTS
        if nxt < units:
            start(i, nxt)
        else:
            @pl.when(i + 1 < n_steps)
            def _():
                start(i + 1, nxt - units)

    def unit_pieces(j):
        g, slot = j % SA_UNITS_PER_SEQ, j % SA_SLOTS
        h0 = g * heads
        st = {"rows": [], "new": [], "outs": []}

        def keys(hh_lo, hh_hi):
            if hh_lo == 0:
                st["s"] = seq_of(i, j)
                if j // SA_UNITS_PER_SEQ not in cols:
                    sel = lane == st["s"]
                    cols[j // SA_UNITS_PER_SEQ] = [
                        jnp.sum(jnp.where(sel, r[...], 0.0), axis=1, keepdims=True)
                        for r in (qt_ref, kt_ref, vt_ref)]
            qc, kc, _ = cols[j // SA_UNITS_PER_SEQ]
            for hh in range(hh_lo, hh_hi):
                st["rows"].append(jnp.sum(kbuf[slot, hh] * head(qc, h0 + hh), axis=0, keepdims=True))
                st["new"].append(jnp.sum(head(qc, h0 + hh) * head(kc, h0 + hh), axis=0,
                                         keepdims=True))

        def softmax():
            lg = jnp.concatenate(st["rows"], axis=0) + sbias_ref[h0:h0 + heads, :]
            lg0 = jnp.concatenate(st["new"], axis=0) + rb0_ref[h0:h0 + heads, :]
            m = jnp.maximum(jnp.max(lg, axis=1, keepdims=True), lg0)
            st["p"] = jnp.exp(lg - m) * cnt_ref[...]
            st["p0"] = len(DILATIONS) * jnp.exp(lg0 - m)
            st["den"] = jnp.sum(st["p"], axis=1, keepdims=True) + st["p0"]

        def values(hh_lo, hh_hi):
            vc = cols[j // SA_UNITS_PER_SEQ][2]
            for hh in range(hh_lo, hh_hi):
                num = jnp.sum(vbuf[slot, hh] * st["p"][hh:hh + 1, :], axis=1, keepdims=True)
                num = num + st["p0"][hh:hh + 1] * head(vc, h0 + hh)
                st["outs"].append(num / st["den"][hh:hh + 1])
            if hh_hi == heads:
                r0, r1 = h0 * HEAD_DIM, (h0 + heads) * HEAD_DIM
                mine = lax.broadcasted_iota(jnp.int32, (r1 - r0, n_seq), 1) == st["s"]
                o_ref[r0:r1, :] = jnp.where(mine, jnp.concatenate(st["outs"], axis=0),
                                            o_ref[r0:r1, :])
                refill(j)

        def whole():
            keys(0, heads)
            softmax()
            values(0, heads)

        return [functools.partial(wait, j), whole]

    return [unit_pieces(j) for j in range(units)]


def _prompt_mixer_pieces(tile, g_ref, c_ref, gx, cx, conv_dst, pool_dst, mixer_w, tm):
    cw_ref, cb_ref, lng_ref, lnb_ref, pw_ref, plw_ref, psc_ref = mixer_w
    ext = tm + 8
    state = {}

    def load():
        keep = (tile > 0).astype(F32)
        gx[0:HALO, :] = gx[0:HALO, :] * keep
        gx[HALO:HALO + tm, :] = g_ref[...]
        cx[0:HALO, :] = cx[0:HALO, :] * keep
        cx[HALO:HALO + tm, :] = c_ref[...]

    def taps(b_lo, b_hi):
        y = state.get("y")
        for b in range(b_lo, b_hi):
            part = None
            for a in range(-(-CONV_WIDTH // 8)):
                lag = 8 * a + b
                if lag >= CONV_WIDTH:
                    continue
                lo = HALO - 8 - 8 * a
                term = cw_ref[CONV_WIDTH - 1 - lag:CONV_WIDTH - lag, :] * gx[lo:lo + ext, :]
                part = term if part is None else part + term
            if b:
                part = pltpu.roll(part, b, 0)
            y = part if y is None else y + part
        state["y"] = y

    def conv_tail():
        conv_dst[...] = _conv_tail(state["y"][8:], cb_ref, lng_ref, lnb_ref, pw_ref).astype(BF16)

    def pool():
        x = c_ref[...]
        win = _pool_lane_consts((tm, D_C))
        run = cx[...]
        tot = jnp.zeros_like(x)
        prev_w = 1
        for w in POOL_WINDOWS:
            assert w == 2 * prev_w
            run = run + pltpu.roll(run, prev_w, 0)
            tot = jnp.where(win == w, run[HALO:], tot)
            prev_w = w
        pos = tile * tm + lax.broadcasted_iota(jnp.int32, (tm, D_C), 0)
        cnt = jnp.minimum(win, pos + 1).astype(F32)
        pool_dst[...] = _pool_tail(tot, x, cnt, plw_ref, psc_ref).astype(BF16)

    def first():
        load()
        taps(0, 1)

    return [first] + [functools.partial(taps, b, b + 1) for b in range(1, 8)] + [conv_tail, pool]


def _post_prompt_kernel(h_ref, attn_ref, g_ref, gh_ref, c_ref, ch_ref,
                        cw_ref, cb_ref, lng_ref, lnb_ref, pw_ref, plw_ref, psc_ref,
                        wout_ref, gf_ref, wgu_ref, wdn_ref, *rest,
                        tm, per_seq, layer, n_steps, sa_seqs):
    sa_in, outs_scratch, sa_scratch = rest[:N_SA_INPUTS], rest[N_SA_INPUTS:-3], rest[-3:]
    out_ref, sa_ref, gx, cx, conv_t, pool_t = outs_scratch
    mixer_w = (cw_ref, cb_ref, lng_ref, lnb_ref, pw_ref, plw_ref, psc_ref)
    i = pl.program_id(0)
    units = _sample_attn_units(i, n_steps, layer, *sa_seqs, sa_in, sa_ref, sa_scratch)
    gx[0:HALO, :] = gh_ref[...]
    cx[0:HALO, :] = ch_ref[...]
    for piece in _prompt_mixer_pieces(i % per_seq, g_ref, c_ref, gx, cx, conv_t, pool_t, mixer_w, tm):
        piece()
    out_ref[...] = _mix_ffn(h_ref[...], attn_ref[...], conv_t[...], pool_t[...],
                            wout_ref, gf_ref, wgu_ref, wdn_ref, _weave_plan(units))


def _post_sample_kernel(h_ref, attn_ref, g_ref, c_ref, cc_ref, cp_ref,
                        cw_ref, cb_ref, lng_ref, lnb_ref, pw_ref, plw_ref, psc_ref,
                        wout_ref, gf_ref, wgu_ref, wdn_ref, out_ref):
    n_hist = CONV_WIDTH - 1
    y = g_ref[...] * cw_ref[n_hist:n_hist + 1, :]
    for j in range(n_hist):
        y = y + cc_ref[0, j] * cw_ref[j:j + 1, :]
    conv = _conv_tail(y, cb_ref, lng_ref, lnb_ref, pw_ref)

    x = c_ref[...]
    win = _pool_lane_consts(x.shape)
    run = x
    tot = jnp.zeros_like(x)
    prev_w = 1
    for w in POOL_WINDOWS:
        for sft in range(prev_w, w):
            run = run + cp_ref[0, POOL_PREFIX - sft]
        tot = jnp.where(win == w, run, tot)
        prev_w = w
    pool = _pool_tail(tot, x, win.astype(F32), plw_ref, psc_ref)

    out_ref[...] = _mix_ffn(h_ref[...], attn_ref[...].T, conv, pool, wout_ref, gf_ref, wgu_ref, wdn_ref)


def _mixer_weight_specs(layer):
    return [_const_spec((CONV_WIDTH, D_B)), _const_spec((1, D_B)), _const_spec((1, D_B)),
            _const_spec((1, D_B)), _const_spec((D_B, D_B)), _const_spec((D_C, D_C)),
            _const_spec((1, D_C)),
            _layer_spec((D_MODEL, D_MODEL), layer), _const_spec((1, D_MODEL)),
            _layer_spec((D_MODEL, 2 * D_FF), layer), _layer_spec((D_FF, D_MODEL), layer)]


def _sample_attn_specs(sample):
    assert len(sample) == N_SA_INPUTS
    n_seq, wbuf = sample[0].shape[1], sample[-1].shape[-1]
    heads = N_HEADS // SA_UNITS_PER_SEQ
    cst = lambda shape: pl.BlockSpec(shape, lambda i: (0,) * len(shape))
    hbm = pl.BlockSpec(memory_space=pl.ANY)
    in_specs = [cst((D_A, n_seq))] * 3 + [cst((N_HEADS, wbuf)), cst((1, wbuf)), cst((N_HEADS, 1)),
                                          cst((D_A, n_seq)), hbm, hbm]
    scratch = [pltpu.VMEM((SA_SLOTS, heads, HEAD_DIM, wbuf), F32),
               pltpu.VMEM((SA_SLOTS, heads, HEAD_DIM, wbuf), F32),
               pltpu.SemaphoreType.DMA((2, SA_SLOTS))]
    return in_specs, cst((D_A, n_seq)), jax.ShapeDtypeStruct((D_A, n_seq), F32), scratch


def _post_prompt_call(layer, h, attn, glu, c, mixw, postw, sample, sa_seqs, tm, seq):
    t = h.shape[0]
    n_steps = t // tm
    per = tm // HALO
    row = lambda w: pl.BlockSpec((tm, w), lambda i: (i, 0))
    halo = lambda w: pl.BlockSpec((HALO, w), lambda i: (jnp.maximum(i * per - 1, 0), 0))
    sa_in, sa_out, sa_shape, sa_scratch = _sample_attn_specs(sample)
    return pl.pallas_call(
        functools.partial(_post_prompt_kernel, tm=tm, per_seq=seq // tm, layer=layer,
                          n_steps=n_steps, sa_seqs=sa_seqs),
        grid=(n_steps,),
        in_specs=[row(D_MODEL), row(D_A), row(D_B), halo(D_B), row(D_C), halo(D_C)]
        + _mixer_weight_specs(layer) + sa_in,
        out_specs=[row(D_MODEL), sa_out],
        out_shape=[jax.ShapeDtypeStruct((t, D_MODEL), F32), sa_shape],
        scratch_shapes=[pltpu.VMEM((HALO + tm, D_B), F32), pltpu.VMEM((HALO + tm, D_C), F32),
                        pltpu.VMEM((tm, D_B), BF16), pltpu.VMEM((tm, D_C), BF16)] + sa_scratch,
        compiler_params=pltpu.CompilerParams(dimension_semantics=("arbitrary",),
                                             vmem_limit_bytes=VMEM_LIMIT),
        name="post_prompt",
    )(h, attn, glu, glu, c, c, *mixw, *postw, *sample)


def _post_sample_call(layer, h, attn, glu, c, cct, cpt, mixw, postw):
    t = h.shape[0]
    full = lambda w: pl.BlockSpec((t, w), lambda i: (0, 0))
    return pl.pallas_call(
        _post_sample_kernel,
        grid=(1,),
        in_specs=[full(D_MODEL), pl.BlockSpec((D_A, t), lambda i: (0, 0)), full(D_B), full(D_C),
                  pl.BlockSpec((1, CONV_WIDTH - 1, t, D_B), lambda i: (layer, 0, 0, 0)),
                  pl.BlockSpec((1, POOL_PREFIX, t, D_C), lambda i: (layer, 0, 0, 0))]
        + _mixer_weight_specs(layer),
        out_specs=full(D_MODEL),
        out_shape=jax.ShapeDtypeStruct((t, D_MODEL), F32),
        compiler_params=pltpu.CompilerParams(dimension_semantics=("arbitrary",),
                                             vmem_limit_bytes=VMEM_LIMIT),
        name="post_sample",
    )(h, attn, glu, c, cct, cpt, *mixw, *postw)


def _bias_prompt_kernel(rb_ref, tbl_ref, out_ref):
    tbl = tbl_ref[0]
    for h in range(N_HEADS):
        def body(b, acc, h=h):
            return jnp.where(tbl == b, rb_ref[b, h], acc)

        acc = lax.fori_loop(0, NUM_BUCKETS, body, jnp.zeros(tbl.shape, F32))
        out_ref[0, h] = jnp.where(tbl < 0, NEG_INF, acc * LOG2E)


def _bias_prompt_call(rel_bias):
    tbl = jnp.asarray(_prompt_bucket_table())
    nbr = len(DILATIONS)
    return pl.pallas_call(
        _bias_prompt_kernel,
        grid=(nbr,),
        in_specs=[pl.BlockSpec(memory_space=pltpu.SMEM),
                  pl.BlockSpec((1, BLK, 2 * BLK), lambda br: (br, 0, 0))],
        out_specs=pl.BlockSpec((1, N_HEADS, BLK, 2 * BLK), lambda br: (br, 0, 0, 0)),
        out_shape=jax.ShapeDtypeStruct((nbr, N_HEADS, BLK, 2 * BLK), F32),
        name="bias_prompt",
    )(rel_bias, tbl)


def _bias_sample_kernel(rb_ref, tbl_ref, out_ref):
    tbl = tbl_ref[...]
    for h in range(N_HEADS):
        def body(b, acc, h=h):
            return jnp.where(tbl == b, rb_ref[b, h], acc)

        acc = lax.fori_loop(0, NUM_BUCKETS, body, jnp.zeros(tbl.shape, F32))
        out_ref[h:h + 1, :] = jnp.where(tbl < 0, NEG_INF, acc)


def _bias_sample_call(rel_bias, bucket):
    return pl.pallas_call(
        _bias_sample_kernel,
        in_specs=[pl.BlockSpec(memory_space=pltpu.SMEM),
                  pl.BlockSpec(bucket.shape, lambda: (0, 0))],
        out_shape=jax.ShapeDtypeStruct((N_HEADS, bucket.shape[1]), F32),
        name="bias_sample",
    )(rel_bias, bucket)


def _rows(start, size, dil):
    return pl.ds(start, size) if dil == 1 else pl.ds(start, size, stride=dil)


def _attn_prompt_kernel(q_ref, k_ref, v_ref, bias_ref, o_ref, m_acc, s_acc, n_acc, *, seq):
    lane_lo = lax.broadcasted_iota(jnp.int32, (BLK, 128), 1) < HEAD_DIM

    def group(br, dil, starts, n_keys, order):
        bias = bias_ref[br, 0, :, 2 * BLK - n_keys:]
        logits, values = [], []
        for q_start, k_start in starts:
            q = q_ref[0, _rows(q_start, BLK, dil), :]
            qs = jnp.concatenate([jnp.where(lane_lo, q, 0.0), jnp.where(lane_lo, 0.0, q)], axis=0)
            kb = k_ref[0, _rows(k_start, n_keys, dil), :].astype(BF16)
            values.append(v_ref[0, _rows(k_start, n_keys, dil), :].astype(BF16))
            logits.append(lax.dot_general(qs.astype(BF16), kb, (((1,), (1,)), ((), ())),
                                          preferred_element_type=F32) + bias)
        stats = []
        ones = jnp.ones((n_keys, 128), BF16)
        for lg, vb in zip(logits, values):
            m = jnp.max(lg, axis=-1, keepdims=True)
            p = jnp.exp2(lg - m)
            pv = _dot(p.astype(BF16), jnp.concatenate([vb, ones], axis=1))
            num, s = pv[:, :128], pv[:, 128:]
            stats.append((jnp.where(lane_lo, m[:BLK], m[BLK:]),
                          jnp.where(lane_lo, s[:BLK], s[BLK:]),
                          jnp.where(lane_lo, num[:BLK], num[BLK:])))
        rows = [_rows(q_start, BLK, dil) for q_start, _ in starts]
        last = order == len(DILATIONS) - 1
        if order == 0:
            for r, (mm, ss, num) in zip(rows, stats):
                m_acc[r, :] = mm
                s_acc[r, :] = ss
                n_acc[r, :] = num
            return
        old = [(m_acc[r, :], s_acc[r, :], n_acc[r, :]) for r in rows]
        for r, (mm, ss, num), (m_old, s_old, n_old) in zip(rows, stats, old):
            m_new = jnp.maximum(m_old, mm)
            a = jnp.exp2(m_old - m_new)
            b = jnp.exp2(mm - m_new)
            s_new = a * s_old + b * ss
            n_new = a * n_old + b * num
            if last:
                o_ref[0, r, :] = n_new / s_new
            else:
                m_acc[r, :] = m_new
                s_acc[r, :] = s_new
                n_acc[r, :] = n_new

    for order, (br, dil) in enumerate(reversed(list(enumerate(DILATIONS)))):
        nb = seq // dil // BLK

        def starts_of(u, first, dil=dil):
            if first:
                return u, u
            lb = u // dil + 1
            r = u % dil
            return r + dil * BLK * lb, r + dil * BLK * (lb - 1)

        for first, n_units in ((True, dil), (False, dil * (nb - 1))):
            n_keys = BLK if first else 2 * BLK
            rem = n_units % ATTN_GROUP
            if rem:
                group(br, dil, [starts_of(u, first) for u in range(rem)], n_keys, order)

            def body(g, carry, br=br, dil=dil, order=order, first=first, rem=rem, n_keys=n_keys,
                     starts_of=starts_of):
                u0 = rem + g * ATTN_GROUP
                group(br, dil, [starts_of(u0 + j, first) for j in range(ATTN_GROUP)], n_keys, order)
                return carry

            lax.fori_loop(0, n_units // ATTN_GROUP, body, 0)


def _attn_prompt_call(q, k, v, bias2):
    bsz, seq, _ = q.shape
    blk = pl.BlockSpec((1, seq, 128), lambda b, hp: (b, 0, hp))
    return pl.pallas_call(
        functools.partial(_attn_prompt_kernel, seq=seq),
        grid=(bsz, N_HEAD_PAIRS),
        in_specs=[blk, blk, blk,
                  pl.BlockSpec((len(DILATIONS), 1, HEADS_PER_VREG * BLK, 2 * BLK),
                               lambda b, hp: (0, hp, 0, 0))],
        out_specs=blk,
        out_shape=jax.ShapeDtypeStruct((bsz, seq, D_A), F32),
        scratch_shapes=[pltpu.VMEM((seq, 128), F32)] * 3,
        compiler_params=pltpu.CompilerParams(dimension_semantics=("arbitrary", "arbitrary"),
                                             vmem_limit_bytes=VMEM_LIMIT),
        name="attn_prompt",
    )(q, k, v, bias2)


def _block_diag(blocks):
    n, r, c = blocks.shape
    eye = jnp.eye(n, dtype=blocks.dtype)
    return (eye[:, None, :, None] * blocks[:, :, None, :]).reshape(n * r, n * c)


def kernel(x_prompt, x_sample, cache_attn_k, cache_attn_v, cache_conv, cache_pool, rel_bias, g_ffn1, w_ffn1_gu, w_ffn1_down, g_mix, w_in, g_q, g_k, conv_w, conv_b, conv_ln_g, conv_ln_b, conv_pw, pool_w, pool_scale, w_out, g_ffn2, w_ffn2_gu, w_ffn2_down):
    bsz, seq, _ = x_prompt.shape
    nseq = x_sample.shape[0]
    depth = g_ffn1.shape[0]
    wbuf = cache_attn_k.shape[2]
    assert x_sample.shape[1] == 1 and wbuf == DILATIONS[-1] * SPAN and nseq == 128
    tm_prompt, tm_post, tm_sample = 512, 512, nseq

    head_of_lane = np.arange(MXU_WIDTH) // HEAD_DIM
    bd = jnp.asarray((head_of_lane[:, None] == head_of_lane[None, :]) / HEAD_DIM, BF16)
    bias_p = _bias_prompt_call(rel_bias).reshape(
        len(DILATIONS), N_HEAD_PAIRS, HEADS_PER_VREG * BLK, 2 * BLK)
    s_bucket, s_count = _sample_tables(wbuf)
    bias_s = _bias_sample_call(rel_bias, jnp.asarray(s_bucket))
    s_count = jnp.asarray(s_count)
    rb0 = rel_bias[0].reshape(N_HEADS, 1)
    ck = jnp.transpose(cache_attn_k, (0, 1, 3, 4, 2))
    cv = jnp.transpose(cache_attn_v, (0, 1, 3, 4, 2))
    cct = jnp.transpose(cache_conv, (0, 2, 1, 3))
    cpt = jnp.transpose(cache_pool, (0, 2, 1, 3))

    hp = x_prompt.reshape(bsz * seq, D_MODEL)
    hs = x_sample.reshape(nseq, D_MODEL)
    outs = {n: [] for n in ("pk", "pv", "pc", "pp", "sk", "sv", "sc", "sp")}
    wgu1, wdn1, win = (w.astype(BF16) for w in (w_ffn1_gu, w_ffn1_down, w_in))
    wgu2, wdn2, wout = (w.astype(BF16) for w in (w_ffn2_gu, w_ffn2_down, w_out))
    for l in range(depth):
        r1 = lambda a: a[l].reshape(1, -1)
        gq = jnp.tile(g_q[l], N_HEADS).reshape(1, D_A)
        gk = jnp.tile(g_k[l], N_HEADS).reshape(1, D_A)
        pw = conv_pw[l].astype(BF16)
        plw = _block_diag(pool_w[l]).astype(BF16)
        mixw = (conv_w[l], r1(conv_b), r1(conv_ln_g), r1(conv_ln_b), pw, plw, r1(pool_scale))
        pre_w = (r1(g_ffn1), wgu1, wdn1, r1(g_mix), win, gq, gk, bd)
        post_w = (wout, r1(g_ffn2), wgu2, wdn2)

        keep = min(wbuf, seq)
        hs, _, sk, sv, sglu, sc, sq_t, sk_t, sv_t = _pre_call(l, hs, *pre_w, tm=tm_sample)
        hp, q, k, v, glu, c, k_t, v_t = _pre_call(l, hp, *pre_w, tm=tm_prompt, seq=seq, keep=keep)

        sq = lambda a: a.reshape(bsz, seq, a.shape[-1])
        attn = _attn_prompt_call(sq(q), sq(k), sq(v), bias_p)
        sample = (sq_t, sk_t, sv_t, bias_s, s_count, rb0, jnp.zeros((D_A, nseq), F32), ck, cv)
        hp, sattn_t = _post_prompt_call(
            l, hp, attn.reshape(bsz * seq, D_A), glu, c, mixw, post_w,
            sample, (0, nseq), tm=tm_post, seq=seq)
        to_cache = lambda a: a.reshape(bsz, N_HEADS, HEAD_DIM, keep).transpose(0, 3, 1, 2)
        outs["pk"].append(to_cache(k_t))
        outs["pv"].append(to_cache(v_t))
        outs["pc"].append(sq(glu)[:, seq - (CONV_WIDTH - 1):])
        outs["pp"].append(sq(c)[:, seq - POOL_PREFIX:])

        hs = _post_sample_call(l, hs, sattn_t, sglu, sc, cct, cpt, mixw, post_w)
        outs["sk"].append(sk.reshape(nseq, 1, N_HEADS, HEAD_DIM))
        outs["sv"].append(sv.reshape(nseq, 1, N_HEADS, HEAD_DIM))
        outs["sc"].append(jnp.concatenate([cache_conv[l][:, 1:], sglu[:, None, :]], axis=1))
        outs["sp"].append(jnp.concatenate([cache_pool[l][:, 1:], sc[:, None, :]], axis=1))

    st = lambda n: jnp.stack(outs[n])
    return (hp.reshape(bsz, seq, D_MODEL), hs.reshape(nseq, 1, D_MODEL),
            st("pk"), st("pv"), st("pc"), st("pp"), st("sk"), st("sv"), st("sc"), st("sp"))
```
